```python
import math
import jax, jax.numpy as jnp
from jax import lax
import numpy as np

D_MODEL = 1024
BATCH = 8
SEQ = 4096
DEPTH = 1

MEM_LEN = 256
MLA_HEADS = 8
MLA_NOPE = 64
MLA_ROPE = 32
MLA_V = 64
MLA_Q_LORA = 256
MLA_KV_LORA = 128
MLA_WIDTH = MLA_HEADS * MLA_V
SB_HEADS = 8
SB_HEAD_DIM = 64
SB_WIDTH = SB_HEADS * SB_HEAD_DIM
MEM_HEADS = 4
MEM_HEAD_DIM = 128
MEM_WIDTH = MEM_HEADS * MEM_HEAD_DIM
N_BRANCHES = 3

BLOCK_Q = 128
ROPE_BASE = 10000.0
RMS_EPS = 1e-6
LN_EPS = 1e-5
DEEPNORM_ALPHA = (2.0 * DEPTH) ** 0.25
DEEPNORM_BETA = (8.0 * DEPTH) ** -0.25

IN_SIZES = [
    MLA_Q_LORA, MLA_KV_LORA, MLA_ROPE, MLA_WIDTH,
    SB_WIDTH, SB_WIDTH, SB_WIDTH, SB_WIDTH,
    MEM_WIDTH, MEM_WIDTH,
]
IN_WIDTH = int(sum(IN_SIZES))
IN_OFFSETS = [int(o) for o in np.cumsum(IN_SIZES)[:-1]]

kernel_name = "hybrid_mla_stickbreaking_memxattn_deepnorm"


def _rms_norm(x, g):
    x32 = x.astype(jnp.float32)
    y = x32 * lax.rsqrt(jnp.mean(x32 * x32, axis=-1, keepdims=True) + RMS_EPS)
    return (y * g.astype(jnp.float32)).astype(x.dtype)


def _layer_norm(x, g, b):
    x32 = x.astype(jnp.float32)
    mu = jnp.mean(x32, axis=-1, keepdims=True)
    xc = x32 - mu
    var = jnp.mean(xc * xc, axis=-1, keepdims=True)
    y = xc * lax.rsqrt(var + LN_EPS) * g.astype(jnp.float32) + b.astype(jnp.float32)
    return y.astype(x.dtype)


def _rope(x, pos):
    half = x.shape[-1] // 2
    freqs = ROPE_BASE ** (-jnp.arange(half, dtype=jnp.float32) / half)
    ang = pos.astype(jnp.float32)[:, None] * freqs[None, :]
    cos = jnp.cos(ang)[None, :, None, :]
    sin = jnp.sin(ang)[None, :, None, :]
    x32 = x.astype(jnp.float32)
    x1, x2 = x32[..., :half], x32[..., half:]
    out = jnp.concatenate([x1 * cos - x2 * sin, x1 * sin + x2 * cos], axis=-1)
    return out.astype(x.dtype)


def _sweep_query_blocks(q, k, v, weights_fn):
    seq = q.shape[1]
    outs = []
    for start in range(0, seq, BLOCK_Q):
        end = start + BLOCK_Q
        scores = jnp.einsum('bqhd,bkhd->bhqk', q[:, start:end], k[:, :end]).astype(jnp.float32)
        q_pos = (start + jnp.arange(BLOCK_Q))[:, None]
        k_pos = jnp.arange(end)[None, :]
        w = weights_fn(scores, q_pos, k_pos)
        outs.append(jnp.einsum('bhqk,bkhd->bqhd', w.astype(v.dtype), v[:, :end]))
    return jnp.concatenate(outs, axis=1)


def _softmax_causal_weights(scale):
    def fn(scores, q_pos, k_pos):
        s = jnp.where(k_pos <= q_pos, scores * scale, jnp.finfo(jnp.float32).min)
        return jax.nn.softmax(s, axis=-1)
    return fn


def _stick_breaking_weights(scale):
    def fn(scores, q_pos, k_pos):
        z = scores * scale
        strict = k_pos < q_pos
        log_beta = jax.nn.log_sigmoid(z)
        log_keep = jnp.where(strict, jax.nn.log_sigmoid(-z), 0.0)
        after = lax.cumsum(log_keep, axis=log_keep.ndim - 1, reverse=True) - log_keep
        return jnp.where(strict, jnp.exp(log_beta + after), 0.0)
    return fn


def _hybrid_layer(x, mem, w_in, w_mem_kv, q_a_gain, w_q_b, kv_a_gain, w_kv_b,
                  w_branch_mla, w_branch_sb, w_branch_mem, w_merge_gate, b_merge_gate,
                  w_out, ln_gain, ln_bias):
    b, s, _ = x.shape
    pos = jnp.arange(s, dtype=jnp.int32)
    proj = x @ w_in
    (c_q, c_kv, k_rope, gate_a, q_b, k_b, v_b, gate_b, q_m, gate_m) = jnp.split(proj, IN_OFFSETS, axis=-1)

    q_a = (_rms_norm(c_q, q_a_gain) @ w_q_b).reshape(b, s, MLA_HEADS, MLA_NOPE + MLA_ROPE)
    kv_a = (_rms_norm(c_kv, kv_a_gain) @ w_kv_b).reshape(b, s, MLA_HEADS, MLA_NOPE + MLA_V)
    q_nope, q_pe = q_a[..., :MLA_NOPE], q_a[..., MLA_NOPE:]
    k_nope, v_a = kv_a[..., :MLA_NOPE], kv_a[..., MLA_NOPE:]
    k_pe = _rope(k_rope.reshape(b, s, 1, MLA_ROPE), pos)
    q_full = jnp.concatenate([q_nope, _rope(q_pe, pos)], axis=-1)
    k_full = jnp.concatenate([k_nope, jnp.broadcast_to(k_pe, (b, s, MLA_HEADS, MLA_ROPE))], axis=-1)
    o_a = _sweep_query_blocks(q_full, k_full, v_a,
                              _softmax_causal_weights(1.0 / math.sqrt(MLA_NOPE + MLA_ROPE)))
    y_a = (o_a.reshape(b, s, MLA_WIDTH) * jax.nn.silu(gate_a)) @ w_branch_mla

    q_sb = q_b.reshape(b, s, SB_HEADS, SB_HEAD_DIM)
    k_sb = k_b.reshape(b, s, SB_HEADS, SB_HEAD_DIM)
    v_sb = v_b.reshape(b, s, SB_HEADS, SB_HEAD_DIM)
    o_b = _sweep_query_blocks(q_sb, k_sb, v_sb, _stick_breaking_weights(1.0 / math.sqrt(SB_HEAD_DIM)))
    y_b = (o_b.reshape(b, s, SB_WIDTH) * jax.nn.silu(gate_b)) @ w_branch_sb

    mem_kv = (mem @ w_mem_kv).reshape(b, mem.shape[1], 2, MEM_HEADS, MEM_HEAD_DIM)
    k_m, v_m = mem_kv[:, :, 0], mem_kv[:, :, 1]
    q_mh = q_m.reshape(b, s, MEM_HEADS, MEM_HEAD_DIM)
    sc = jnp.einsum('bshd,bmhd->bhsm', q_mh, k_m).astype(jnp.float32) / math.sqrt(MEM_HEAD_DIM)
    p_m = jax.nn.softmax(sc, axis=-1).astype(v_m.dtype)
    o_m = jnp.einsum('bhsm,bmhd->bshd', p_m, v_m).reshape(b, s, MEM_WIDTH)
    y_m = (o_m * jax.nn.silu(gate_m)) @ w_branch_mem

    g = jax.nn.sigmoid(x @ w_merge_gate + b_merge_gate)
    g_a, g_b, g_m = jnp.split(g, N_BRANCHES, axis=-1)
    merged = g_a * y_a + g_b * y_b + g_m * y_m
    out = merged @ w_out

    return _layer_norm(DEEPNORM_ALPHA * x + out, ln_gain, ln_bias)


def setup_inputs(seed: int = 0) -> dict:
    key = jax.random.key(seed)
    ks = jax.random.split(key, 18)
    f32 = jnp.float32

    def nrm(k, shape, fan_in, gain=1.0):
        return jax.random.normal(k, shape, f32) * (gain * fan_in ** -0.5)

    L = DEPTH
    return {
        "x": jax.random.normal(ks[0], (BATCH, SEQ, D_MODEL), f32),
        "mem": jax.random.normal(ks[1], (BATCH, MEM_LEN, D_MODEL), f32),
        "w_in": nrm(ks[2], (L, D_MODEL, IN_WIDTH), D_MODEL),
        "w_mem_kv": nrm(ks[3], (L, D_MODEL, 2 * MEM_WIDTH), D_MODEL),
        "q_a_gain": 1.0 + 0.01 * jax.random.normal(ks[4], (L, MLA_Q_LORA), f32),
        "w_q_b": nrm(ks[5], (L, MLA_Q_LORA, MLA_HEADS * (MLA_NOPE + MLA_ROPE)), MLA_Q_LORA),
        "kv_a_gain": 1.0 + 0.01 * jax.random.normal(ks[6], (L, MLA_KV_LORA), f32),
        "w_kv_b": nrm(ks[7], (L, MLA_KV_LORA, MLA_HEADS * (MLA_NOPE + MLA_V)), MLA_KV_LORA),
        "w_branch_mla": nrm(ks[8], (L, MLA_WIDTH, D_MODEL), MLA_WIDTH, DEEPNORM_BETA),
        "w_branch_sb": nrm(ks[9], (L, SB_WIDTH, D_MODEL), SB_WIDTH, DEEPNORM_BETA),
        "w_branch_mem": nrm(ks[10], (L, MEM_WIDTH, D_MODEL), MEM_WIDTH, DEEPNORM_BETA),
        "w_merge_gate": nrm(ks[11], (L, D_MODEL, N_BRANCHES * D_MODEL), D_MODEL),
        "b_merge_gate": 0.01 * jax.random.normal(ks[12], (L, N_BRANCHES * D_MODEL), f32),
        "w_out": nrm(ks[13], (L, D_MODEL, D_MODEL), D_MODEL, DEEPNORM_BETA),
        "ln_gain": 1.0 + 0.01 * jax.random.normal(ks[14], (L, D_MODEL), f32),
        "ln_bias": 0.01 * jax.random.normal(ks[15], (L, D_MODEL), f32),
    }


def reference(x, mem, w_in, w_mem_kv, q_a_gain, w_q_b, kv_a_gain, w_kv_b,
              w_branch_mla, w_branch_sb, w_branch_mem, w_merge_gate, b_merge_gate,
              w_out, ln_gain, ln_bias):
    h = x
    for l in range(DEPTH):
        h = _hybrid_layer(h, mem, w_in[l], w_mem_kv[l], q_a_gain[l], w_q_b[l], kv_a_gain[l], w_kv_b[l],
                          w_branch_mla[l], w_branch_sb[l], w_branch_mem[l], w_merge_gate[l],
                          b_merge_gate[l], w_out[l], ln_gain[l], ln_bias[l])
    return h
```

```python
import functools
import math

import numpy as np
import jax
import jax.numpy as jnp
from jax import lax
from jax.experimental import pallas as pl
from jax.experimental.pallas import tpu as pltpu

F32 = jnp.float32
BF16 = jnp.bfloat16

D_MODEL = 1024
MEM_LEN = 256
MLA_HEADS, MLA_NOPE, MLA_ROPE, MLA_V = 8, 64, 32, 64
MLA_Q_LORA, MLA_KV_LORA = 256, 128
SB_HEADS, SB_HEAD_DIM = 8, 64
MEM_HEADS, MEM_HEAD_DIM = 4, 128
WIDTH = 512
ROPE_BASE = 10000.0
RMS_EPS = 1e-6
LN_EPS = 1e-5
DEPTH = 1
DEEPNORM_ALPHA = (2.0 * DEPTH) ** 0.25

_OFF = np.cumsum([0, MLA_Q_LORA, MLA_KV_LORA, MLA_ROPE, WIDTH, WIDTH, WIDTH, WIDTH, WIDTH, WIDTH, WIDTH])
(O_CQ, O_CKV, O_KR, O_GA, O_QB, O_KB, O_VB, O_GB, O_QM, O_GM, O_END) = [int(v) for v in _OFF]

TQ = 256
HEAD_PAD = 128
VMEM_LIMIT = 56 * 1024 * 1024
NEG = -1e30

R_CQ, R_CKV, R_GA, R_QB, R_VB, R_GB, R_QM, R_GM, R_END = [
    int(v) for v in np.cumsum([0, MLA_Q_LORA, MLA_KV_LORA, WIDTH, WIDTH, WIDTH, WIDTH, WIDTH, WIDTH])]
C_CKV, C_G1, C_G2, C_KB, C_END = 0, 128, 256, 384, 896

_NT = (((1,), (1,)), ((), ()))
_TN = (((0,), (0,)), ((), ()))


def _sigmoid(t):
    return 1.0 / (1.0 + jnp.exp(-t))


def _silu(t):
    return t * _sigmoid(t)


def _mem_kv_kernel(mem_ref, wk_ref, wvT_ref, km_ref, vmT_ref):
    mb = mem_ref[0].astype(BF16)
    km_ref[0] = jnp.dot(mb, wk_ref[...], preferred_element_type=F32).astype(BF16)
    vmT_ref[0] = lax.dot_general(wvT_ref[...], mb, _NT, preferred_element_type=F32).astype(BF16)


def _in_proj_kernel(x_ref, wT_ref, wS_ref, qgain_ref, kvgain_col_ref, kvgain_row_ref, wqT_ref, wvT_ref,
                    wk_ref, cosq_ref, sinq_ref, cosk_ref, sink_ref, km_ref, vmT_ref,
                    qa_ref, ka_ref, va_ref, qb_ref, kb_ref, vb_ref, ga_ref, gb_ref, om_ref):
    xb = x_ref[0].astype(BF16)
    pT = lax.dot_general(wT_ref[...], xb, _NT, preferred_element_type=F32)
    pS = jnp.dot(xb, wS_ref[...], preferred_element_type=F32)

    cq = pT[R_CQ:R_CKV]
    nq = cq * lax.rsqrt(jnp.mean(cq * cq, axis=0, keepdims=True) + RMS_EPS) * qgain_ref[...]
    qaT = jnp.dot(wqT_ref[...], nq.astype(BF16), preferred_element_type=F32)
    scale = 1.0 / math.sqrt(MLA_NOPE + MLA_ROPE)
    n_nope = MLA_HEADS * MLA_NOPE
    half = MLA_ROPE // 2
    x1 = qaT[n_nope:n_nope + MLA_HEADS * half]
    x2 = qaT[n_nope + MLA_HEADS * half:]
    cq_t, sq_t = cosq_ref[...], sinq_ref[...]
    r1 = (x1 * cq_t - x2 * sq_t) * scale
    r2 = (x1 * sq_t + x2 * cq_t) * scale
    nope = qaT[:n_nope] * scale
    zpad = jnp.zeros((HEAD_PAD - MLA_NOPE - MLA_ROPE, TQ), BF16)
    for h in range(MLA_HEADS):
        qa_ref[0, h, 0, 0:MLA_NOPE, :] = nope[h * MLA_NOPE:(h + 1) * MLA_NOPE].astype(BF16)
        qa_ref[0, h, 0, MLA_NOPE:MLA_NOPE + half, :] = r1[h * half:(h + 1) * half].astype(BF16)
        qa_ref[0, h, 0, MLA_NOPE + half:MLA_NOPE + MLA_ROPE, :] = r2[h * half:(h + 1) * half].astype(BF16)
        qa_ref[0, h, 0, MLA_NOPE + MLA_ROPE:, :] = zpad

    ckvT = pT[R_CKV:R_GA]
    nkvT = ckvT * lax.rsqrt(jnp.mean(ckvT * ckvT, axis=0, keepdims=True) + RMS_EPS) * kvgain_col_ref[...]
    vaT = jnp.dot(wvT_ref[...], nkvT.astype(BF16), preferred_element_type=F32)
    for h in range(MLA_HEADS):
        va_ref[0, h, 0] = vaT[h * MLA_V:(h + 1) * MLA_V].astype(BF16)

    ckv = pS[:, C_CKV:C_G1]
    nkv = ckv * lax.rsqrt(jnp.mean(ckv * ckv, axis=1, keepdims=True) + RMS_EPS) * kvgain_row_ref[...]
    kpe = pS[:, C_G1:C_G2] * cosk_ref[...] + pS[:, C_G2:C_KB] * sink_ref[...]
    kin = jnp.concatenate([nkv.astype(BF16), kpe.astype(BF16)], axis=1)
    ka_ref[0] = jnp.dot(kin, wk_ref[...], preferred_element_type=F32).astype(BF16)

    kb_ref[0] = pS[:, C_KB:C_END].astype(BF16)
    qbT = pT[R_QB:R_VB]
    vbT = pT[R_VB:R_GB]
    zhalf = jnp.zeros((HEAD_PAD - SB_HEAD_DIM, TQ), BF16)
    for h in range(SB_HEADS):
        lo = (h % 2) * SB_HEAD_DIM
        other = SB_HEAD_DIM - lo
        qb_ref[0, h, 0, lo:lo + SB_HEAD_DIM, :] = qbT[h * SB_HEAD_DIM:(h + 1) * SB_HEAD_DIM].astype(BF16)
        qb_ref[0, h, 0, other:other + SB_HEAD_DIM, :] = zhalf
        vb_ref[0, h, 0] = vbT[h * SB_HEAD_DIM:(h + 1) * SB_HEAD_DIM].astype(BF16)

    ga_ref[0, 0] = pT[R_GA:R_QB]
    gb_ref[0, 0] = pT[R_GB:R_QM]

    qmT = pT[R_QM:R_GM]
    gmT = pT[R_GM:R_END]
    km = km_ref[0]
    vmT = vmT_ref[0]
    inv_sqrt_d = 1.0 / math.sqrt(MEM_HEAD_DIM)
    for h in range(MEM_HEADS):
        sl = slice(h * MEM_HEAD_DIM, (h + 1) * MEM_HEAD_DIM)
        s = jnp.dot(km[:, sl], qmT[sl].astype(BF16), preferred_element_type=F32) * inv_sqrt_d
        e = jnp.exp(s - jnp.max(s, axis=0, keepdims=True))
        inv_l = 1.0 / jnp.sum(e, axis=0, keepdims=True)
        o = jnp.dot(vmT[sl], e.astype(BF16), preferred_element_type=F32) * inv_l
        om_ref[0, 0, sl, :] = (o * _silu(gmT[sl])).astype(BF16)


def _mla_attn_kernel(q_ref, k_ref, v_ref, o_ref):
    n_tiles = q_ref.shape[2]
    key_idx = lax.broadcasted_iota(jnp.int32, (TQ, TQ), 0)
    qry_idx = lax.broadcasted_iota(jnp.int32, (TQ, TQ), 1)
    causal = key_idx <= qry_idx

    def q_body(qi, carry):
        qT = q_ref[0, 0, qi]

        def scores(kj):
            start = pl.multiple_of(kj * TQ, TQ)
            return jnp.dot(k_ref[0, pl.ds(start, TQ), :], qT, preferred_element_type=F32)

        s = jnp.where(causal, scores(qi), NEG)
        m = jnp.max(s, axis=0, keepdims=True)
        p = jnp.exp(s - m)
        l = jnp.sum(p, axis=0, keepdims=True)
        acc = jnp.dot(v_ref[0, 0, qi], p.astype(BF16), preferred_element_type=F32)

        def k_body(kj, c):
            m, l, acc = c
            s = scores(kj)
            m_new = jnp.maximum(m, jnp.max(s, axis=0, keepdims=True))
            alpha = jnp.exp(m - m_new)
            p = jnp.exp(s - m_new)
            l = alpha * l + jnp.sum(p, axis=0, keepdims=True)
            acc = alpha * acc + jnp.dot(v_ref[0, 0, kj], p.astype(BF16), preferred_element_type=F32)
            return m_new, l, acc

        m, l, acc = lax.fori_loop(0, qi, k_body, (m, l, acc))
        o_ref[0, 0, qi] = acc * (1.0 / l)
        return carry

    lax.fori_loop(0, n_tiles, q_body, 0)


def _sb_attn_kernel(q_ref, k_ref, v_ref, tri_ref, o_ref):
    n_tiles = q_ref.shape[2]
    key_idx = lax.broadcasted_iota(jnp.int32, (TQ, TQ), 0)
    qry_idx = lax.broadcasted_iota(jnp.int32, (TQ, TQ), 1)
    strict = key_idx < qry_idx

    def q_body(qi, carry):
        qT = q_ref[0, 0, qi]

        def tile(kj, run, acc, masked):
            start = pl.multiple_of(kj * TQ, TQ)
            z = jnp.dot(k_ref[0, pl.ds(start, TQ), :], qT, preferred_element_type=F32)
            sp = jnp.maximum(z, 0.0) + jnp.log(1.0 + jnp.exp(-jnp.abs(z)))
            if masked:
                sp = jnp.where(strict, sp, 0.0)
            hi = sp.astype(BF16)
            lo = (sp - hi.astype(F32)).astype(BF16)
            suffix = jnp.dot(tri_ref[...], jnp.concatenate([hi, lo], axis=0),
                             preferred_element_type=F32)
            arg = z - suffix - run
            if masked:
                arg = jnp.where(strict, arg, NEG)
            w = jnp.exp(arg)
            acc = acc + jnp.dot(v_ref[0, 0, kj], w.astype(BF16), preferred_element_type=F32)
            return run + suffix[0:1, :], acc

        run, acc = tile(qi, jnp.zeros((1, TQ), F32), jnp.zeros((SB_HEAD_DIM, TQ), F32), True)

        def k_body(i, c):
            return tile(qi - 1 - i, c[0], c[1], False)

        run, acc = lax.fori_loop(0, qi, k_body, (run, acc))
        o_ref[0, 0, qi] = acc
        return carry

    lax.fori_loop(0, n_tiles, q_body, 0)


def _out_block_kernel(x_ref, oa_ref, ob_ref, ga_ref, gb_ref, om_ref, waT_ref, wbT_ref, wmT_ref,
                      wgT_ref, bg_ref, wout_ref, lng_ref, lnb_ref, y_ref):
    x = x_ref[0]
    oa = oa_ref[0, :, 0].reshape(WIDTH, TQ)
    ob = ob_ref[0, :, 0].reshape(WIDTH, TQ)
    ha = (oa * _silu(ga_ref[0, 0])).astype(BF16)
    hb = (ob * _silu(gb_ref[0, 0])).astype(BF16)
    yaT = jnp.dot(waT_ref[...], ha, preferred_element_type=F32)
    ybT = jnp.dot(wbT_ref[...], hb, preferred_element_type=F32)
    ymT = jnp.dot(wmT_ref[...], om_ref[0, 0], preferred_element_type=F32)
    gT = _sigmoid(lax.dot_general(wgT_ref[...], x.astype(BF16), _NT, preferred_element_type=F32)
                  + bg_ref[...])
    merged = gT[0:D_MODEL] * yaT + gT[D_MODEL:2 * D_MODEL] * ybT + gT[2 * D_MODEL:] * ymT
    out = lax.dot_general(merged.astype(BF16), wout_ref[...], _TN, preferred_element_type=F32)
    r = DEEPNORM_ALPHA * x + out
    mu = jnp.mean(r, axis=1, keepdims=True)
    rc = r - mu
    var = jnp.mean(rc * rc, axis=1, keepdims=True)
    y_ref[0] = rc * lax.rsqrt(var + LN_EPS) * lng_ref[...] + lnb_ref[...]


def _const_spec(shape):
    return pl.BlockSpec(shape, lambda *_: (0,) * len(shape))


def _params(n_axes):
    return pltpu.CompilerParams(dimension_semantics=("parallel",) * n_axes, vmem_limit_bytes=VMEM_LIMIT)


def _layer(x, mem, w_in, w_mem_kv, q_a_gain, w_q_b, kv_a_gain, w_kv_b, w_branch_mla, w_branch_sb,
           w_branch_mem, w_merge_gate, b_merge_gate, w_out, ln_gain, ln_bias):
    B, S, D = x.shape
    assert D == D_MODEL and S % TQ == 0 and mem.shape == (B, MEM_LEN, D)
    NT = S // TQ
    half = MLA_ROPE // 2

    t_cols = np.concatenate([np.arange(O_CQ, O_KR), np.arange(O_GA, O_KB), np.arange(O_VB, O_END)])
    t_scale = np.ones((R_END, 1), np.float32)
    t_scale[R_QB:R_VB] = 1.0 / math.sqrt(SB_HEAD_DIM)
    wT = (w_in[:, t_cols].T * t_scale).astype(BF16)
    zeros96 = jnp.zeros((D, HEAD_PAD - MLA_ROPE), F32)
    w_rope = w_in[:, O_KR:O_GA]
    w_rot = jnp.concatenate([-w_rope[:, half:], w_rope[:, :half]], axis=1)
    wS = jnp.concatenate([w_in[:, O_CKV:O_KR], w_rope, zeros96, w_rot, zeros96, w_in[:, O_KB:O_VB]],
                         axis=1).astype(BF16)

    qd = MLA_NOPE + MLA_ROPE
    q_rows = np.concatenate(
        [np.concatenate([h * qd + np.arange(MLA_NOPE) for h in range(MLA_HEADS)]),
         np.concatenate([h * qd + MLA_NOPE + np.arange(half) for h in range(MLA_HEADS)]),
         np.concatenate([h * qd + MLA_NOPE + half + np.arange(half) for h in range(MLA_HEADS)])])
    wqT = w_q_b[:, q_rows].T.astype(BF16)
    kvd = MLA_NOPE + MLA_V
    v_rows = np.concatenate([h * kvd + MLA_NOPE + np.arange(MLA_V) for h in range(MLA_HEADS)])
    wvT = w_kv_b[:, v_rows].T.astype(BF16)
    knope_mask = np.zeros((1, MLA_HEADS * kvd), np.float32)
    place = np.zeros((HEAD_PAD, MLA_HEADS * HEAD_PAD), np.float32)
    for h in range(MLA_HEADS):
        knope_mask[0, h * kvd:h * kvd + MLA_NOPE] = 1.0
        place[np.arange(MLA_ROPE), h * HEAD_PAD + MLA_NOPE + np.arange(MLA_ROPE)] = 1.0
    wk = jnp.concatenate([w_kv_b * knope_mask, jnp.asarray(place)], axis=0).astype(BF16)

    qgain = q_a_gain.reshape(MLA_Q_LORA, 1)
    kvgain_col = kv_a_gain.reshape(MLA_KV_LORA, 1)
    kvgain_row = kv_a_gain.reshape(1, MLA_KV_LORA)

    freqs = ROPE_BASE ** (-jnp.arange(half, dtype=F32) / half)
    ang = jnp.arange(S, dtype=jnp.int32).astype(F32)[:, None] * freqs[None, :]
    cos, sin = jnp.cos(ang), jnp.sin(ang)
    cosq = jnp.tile(cos.T, (MLA_HEADS, 1))
    sinq = jnp.tile(sin.T, (MLA_HEADS, 1))
    zeros_k = jnp.zeros((S, HEAD_PAD - MLA_ROPE), F32)
    cosk = jnp.concatenate([cos, cos, zeros_k], axis=1)
    sink = jnp.concatenate([sin, sin, zeros_k], axis=1)

    wkm = w_mem_kv[:, :WIDTH].astype(BF16)
    wvmT = w_mem_kv[:, WIDTH:].T.astype(BF16)
    km, vmT = pl.pallas_call(
        _mem_kv_kernel,
        grid=(B,),
        in_specs=[pl.BlockSpec((1, MEM_LEN, D), lambda b: (b, 0, 0)),
                  _const_spec((D, WIDTH)), _const_spec((WIDTH, D))],
        out_specs=[pl.BlockSpec((1, MEM_LEN, WIDTH), lambda b: (b, 0, 0)),
                   pl.BlockSpec((1, WIDTH, MEM_LEN), lambda b: (b, 0, 0))],
        out_shape=[jax.ShapeDtypeStruct((B, MEM_LEN, WIDTH), BF16),
                   jax.ShapeDtypeStruct((B, WIDTH, MEM_LEN), BF16)],
        compiler_params=_params(1),
        name="mem_kv",
    )(mem, wkm, wvmT)

    head_q = lambda: pl.BlockSpec((1, MLA_HEADS, 1, HEAD_PAD, TQ), lambda b, t: (b, 0, t, 0, 0))
    head_v = lambda: pl.BlockSpec((1, MLA_HEADS, 1, MLA_V, TQ), lambda b, t: (b, 0, t, 0, 0))
    wide = lambda: pl.BlockSpec((1, 1, WIDTH, TQ), lambda b, t: (b, t, 0, 0))
    qa, ka, va, qb, kb, vb, ga, gb, om = pl.pallas_call(
        _in_proj_kernel,
        grid=(B, NT),
        in_specs=[pl.BlockSpec((1, TQ, D), lambda b, t: (b, t, 0)),
                  _const_spec((R_END, D)), _const_spec((D, C_END)),
                  _const_spec((MLA_Q_LORA, 1)), _const_spec((MLA_KV_LORA, 1)), _const_spec((1, MLA_KV_LORA)),
                  _const_spec((MLA_HEADS * qd, MLA_Q_LORA)), _const_spec((WIDTH, MLA_KV_LORA)),
                  _const_spec((2 * HEAD_PAD, MLA_HEADS * HEAD_PAD)),
                  pl.BlockSpec((MLA_HEADS * half, TQ), lambda b, t: (0, t)),
                  pl.BlockSpec((MLA_HEADS * half, TQ), lambda b, t: (0, t)),
                  pl.BlockSpec((TQ, HEAD_PAD), lambda b, t: (t, 0)),
                  pl.BlockSpec((TQ, HEAD_PAD), lambda b, t: (t, 0)),
                  pl.BlockSpec((1, MEM_LEN, WIDTH), lambda b, t: (b, 0, 0)),
                  pl.BlockSpec((1, WIDTH, MEM_LEN), lambda b, t: (b, 0, 0))],
        out_specs=[head_q(),
                   pl.BlockSpec((1, TQ, MLA_HEADS * HEAD_PAD), lambda b, t: (b, t, 0)),
                   head_v(),
                   head_q(),
                   pl.BlockSpec((1, TQ, WIDTH), lambda b, t: (b, t, 0)),
                   head_v(),
                   wide(), wide(), wide()],
        out_shape=[jax.ShapeDtypeStruct((B, MLA_HEADS, NT, HEAD_PAD, TQ), BF16),
                   jax.ShapeDtypeStruct((B, S, MLA_HEADS * HEAD_PAD), BF16),
                   jax.ShapeDtypeStruct((B, MLA_HEADS, NT, MLA_V, TQ), BF16),
                   jax.ShapeDtypeStruct((B, SB_HEADS, NT, HEAD_PAD, TQ), BF16),
                   jax.ShapeDtypeStruct((B, S, WIDTH), BF16),
                   jax.ShapeDtypeStruct((B, SB_HEADS, NT, SB_HEAD_DIM, TQ), BF16),
                   jax.ShapeDtypeStruct((B, NT, WIDTH, TQ), F32),
                   jax.ShapeDtypeStruct((B, NT, WIDTH, TQ), F32),
                   jax.ShapeDtypeStruct((B, NT, WIDTH, TQ), BF16)],
        compiler_params=_params(2),
        name="in_proj",
    )(x, wT, wS, qgain, kvgain_col, kvgain_row, wqT, wvT, wk, cosq, sinq, cosk, sink, km, vmT)

    q_spec = pl.BlockSpec((1, 1, NT, HEAD_PAD, TQ), lambda b, h: (b, h, 0, 0, 0))
    v_spec = pl.BlockSpec((1, 1, NT, MLA_V, TQ), lambda b, h: (b, h, 0, 0, 0))
    o_shape = jax.ShapeDtypeStruct((B, MLA_HEADS, NT, MLA_V, TQ), F32)
    oa = pl.pallas_call(
        _mla_attn_kernel,
        grid=(B, MLA_HEADS),
        in_specs=[q_spec, pl.BlockSpec((1, S, HEAD_PAD), lambda b, h: (b, 0, h)), v_spec],
        out_specs=v_spec,
        out_shape=o_shape,
        compiler_params=_params(2),
        name="mla_attn",
    )(qa, ka, va)

    tri_np = np.triu(np.ones((TQ, TQ), np.float32))
    tri = jnp.asarray(np.concatenate([tri_np, tri_np], axis=1), dtype=BF16)
    ob = pl.pallas_call(
        _sb_attn_kernel,
        grid=(B, SB_HEADS),
        in_specs=[q_spec, pl.BlockSpec((1, S, HEAD_PAD), lambda b, h: (b, 0, h // 2)), v_spec,
                  _const_spec((TQ, 2 * TQ))],
        out_specs=v_spec,
        out_shape=o_shape,
        compiler_params=_params(2),
        name="sb_attn",
    )(qb, kb, vb, tri)

    o_in = lambda: pl.BlockSpec((1, MLA_HEADS, 1, MLA_V, TQ), lambda b, t: (b, 0, t, 0, 0))
    y = pl.pallas_call(
        _out_block_kernel,
        grid=(B, NT),
        in_specs=[pl.BlockSpec((1, TQ, D), lambda b, t: (b, t, 0)),
                  o_in(), o_in(), wide(), wide(), wide(),
                  _const_spec((D, WIDTH)), _const_spec((D, WIDTH)), _const_spec((D, WIDTH)),
                  _const_spec((3 * D, D)), _const_spec((3 * D, 1)), _const_spec((D, D)),
                  _const_spec((1, D)), _const_spec((1, D))],
        out_specs=pl.BlockSpec((1, TQ, D), lambda b, t: (b, t, 0)),
        out_shape=jax.ShapeDtypeStruct((B, S, D), F32),
        compiler_params=_params(2),
        name="out_block",
    )(x, oa, ob, ga, gb, om,
      w_branch_mla.T.astype(BF16), w_branch_sb.T.astype(BF16), w_branch_mem.T.astype(BF16),
      w_merge_gate.T.astype(BF16), b_merge_gate.reshape(3 * D, 1), w_out.astype(BF16),
      ln_gain.reshape(1, D), ln_bias.reshape(1, D))
    return y


def kernel(x, mem, w_in, w_mem_kv, q_a_gain, w_q_b, kv_a_gain, w_kv_b, w_branch_mla, w_branch_sb,
           w_branch_mem, w_merge_gate, b_merge_gate, w_out, ln_gain, ln_bias):
    h = x
    for l in range(w_in.shape[0]):
        h = _layer(h, mem, w_in[l], w_mem_kv[l], q_a_gain[l], w_q_b[l], kv_a_gain[l], w_kv_b[l],
                   w_branch_mla[l], w_branch_sb[l], w_branch_mem[l], w_merge_gate[l], b_merge_gate[l],
                   w_out[l], ln_gain[l], ln_bias[l])
    return h
```

```python
import functools
import math

import numpy as np
import jax
import jax.numpy as jnp
from jax import lax
from jax.experimental import pallas as pl
from jax.experimental.pallas import tpu as pltpu

F32 = jnp.float32
BF16 = jnp.bfloat16

D_MODEL = 1024
MEM_LEN = 256
MLA_HEADS, MLA_NOPE, MLA_ROPE, MLA_V = 8, 64, 32, 64
MLA_Q_LORA, MLA_KV_LORA = 256, 128
SB_HEADS, SB_HEAD_DIM = 8, 64
MEM_HEADS, MEM_HEAD_DIM = 4, 128
WIDTH = 512
ROPE_BASE = 10000.0
RMS_EPS = 1e-6
LN_EPS = 1e-5
DEPTH = 1
DEEPNORM_ALPHA = (2.0 * DEPTH) ** 0.25

_OFF = np.cumsum([0, MLA_Q_LORA, MLA_KV_LORA, MLA_ROPE, WIDTH, WIDTH, WIDTH, WIDTH, WIDTH, WIDTH, WIDTH])
(O_CQ, O_CKV, O_KR, O_GA, O_QB, O_KB, O_VB, O_GB, O_QM, O_GM, O_END) = [int(v) for v in _OFF]

TQ = 256
HEAD_PAD = 128
VMEM_LIMIT = 56 * 1024 * 1024
NEG = -1e30
HEADS_PER_STEP = 4
MLA_V_ROWS = 80
LOG2E = math.log2(math.e)
SB_DEAD_LOG2 = 152.0

R_CQ, R_CKV, R_GA, R_QB, R_VB, R_GB, R_QM, R_GM, R_END = [
    int(v) for v in np.cumsum([0, MLA_Q_LORA, MLA_KV_LORA, WIDTH, WIDTH, WIDTH, WIDTH, WIDTH, WIDTH])]
C_CKV, C_G1, C_G2, C_KB, C_END = 0, 128, 256, 384, 896

_NT = (((1,), (1,)), ((), ()))
_TN = (((0,), (0,)), ((), ()))


def _sigmoid(t):
    return 1.0 / (1.0 + jnp.exp(-t))


def _silu(t):
    return t * _sigmoid(t)


def _mem_kv_kernel(mem_ref, wk_ref, wvT_ref, km_ref, vmT_ref):
    mb = mem_ref[0].astype(BF16)
    km_ref[0] = jnp.dot(mb, wk_ref[...], preferred_element_type=F32).astype(BF16)
    vmT_ref[0] = lax.dot_general(wvT_ref[...], mb, _NT, preferred_element_type=F32).astype(BF16)


def _in_proj_kernel(x_ref, wT_ref, wS_ref, qgain_ref, kvgain_col_ref, kvgain_row_ref, wqT_ref, wvT_ref,
                    wk_ref, cosq_ref, sinq_ref, cosk_ref, sink_ref, km_ref, vmT_ref,
                    qa_ref, ka_ref, va_ref, qb_ref, kb_ref, vb_ref, ga_ref, gb_ref, om_ref):
    xb = x_ref[0].astype(BF16)
    pT = lax.dot_general(wT_ref[...], xb, _NT, preferred_element_type=F32)
    pS = jnp.dot(xb, wS_ref[...], preferred_element_type=F32)

    cq = pT[R_CQ:R_CKV]
    nq = cq * lax.rsqrt(jnp.mean(cq * cq, axis=0, keepdims=True) + RMS_EPS) * qgain_ref[...]
    qaT = jnp.dot(wqT_ref[...], nq.astype(BF16), preferred_element_type=F32)
    scale = LOG2E / math.sqrt(MLA_NOPE + MLA_ROPE)
    n_nope = MLA_HEADS * MLA_NOPE
    half = MLA_ROPE // 2
    x1 = qaT[n_nope:n_nope + MLA_HEADS * half]
    x2 = qaT[n_nope + MLA_HEADS * half:]
    cq_t, sq_t = cosq_ref[...], sinq_ref[...]
    r1 = (x1 * cq_t - x2 * sq_t) * scale
    r2 = (x1 * sq_t + x2 * cq_t) * scale
    nope = qaT[:n_nope] * scale
    zpad = jnp.zeros((HEAD_PAD - MLA_NOPE - MLA_ROPE, TQ), BF16)
    for h in range(MLA_HEADS):
        qa_ref[0, h, 0, 0:MLA_NOPE, :] = nope[h * MLA_NOPE:(h + 1) * MLA_NOPE].astype(BF16)
        qa_ref[0, h, 0, MLA_NOPE:MLA_NOPE + half, :] = r1[h * half:(h + 1) * half].astype(BF16)
        qa_ref[0, h, 0, MLA_NOPE + half:MLA_NOPE + MLA_ROPE, :] = r2[h * half:(h + 1) * half].astype(BF16)
        qa_ref[0, h, 0, MLA_NOPE + MLA_ROPE:, :] = zpad

    ckvT = pT[R_CKV:R_GA]
    nkvT = ckvT * lax.rsqrt(jnp.mean(ckvT * ckvT, axis=0, keepdims=True) + RMS_EPS) * kvgain_col_ref[...]
    vaT = jnp.dot(wvT_ref[...], nkvT.astype(BF16), preferred_element_type=F32)
    row = lax.broadcasted_iota(jnp.int32, (MLA_V_ROWS - MLA_V, TQ), 0)
    ones_row = jnp.where(row == 0, 1.0, 0.0).astype(BF16)
    for h in range(MLA_HEADS):
        va_ref[0, h, 0, 0:MLA_V, :] = vaT[h * MLA_V:(h + 1) * MLA_V].astype(BF16)
        va_ref[0, h, 0, MLA_V:, :] = ones_row

    ckv = pS[:, C_CKV:C_G1]
    nkv = ckv * lax.rsqrt(jnp.mean(ckv * ckv, axis=1, keepdims=True) + RMS_EPS) * kvgain_row_ref[...]
    kpe = pS[:, C_G1:C_G2] * cosk_ref[...] + pS[:, C_G2:C_KB] * sink_ref[...]
    kin = jnp.concatenate([nkv.astype(BF16), kpe.astype(BF16)], axis=1)
    ka_ref[0] = jnp.dot(kin, wk_ref[...], preferred_element_type=F32).astype(BF16)

    kb_ref[0] = pS[:, C_KB:C_END].astype(BF16)
    qbT = pT[R_QB:R_VB]
    vbT = pT[R_VB:R_GB]
    zhalf = jnp.zeros((HEAD_PAD - SB_HEAD_DIM, TQ), BF16)
    for h in range(SB_HEADS):
        lo = (h % 2) * SB_HEAD_DIM
        other = SB_HEAD_DIM - lo
        qb_ref[0, h, 0, lo:lo + SB_HEAD_DIM, :] = qbT[h * SB_HEAD_DIM:(h + 1) * SB_HEAD_DIM].astype(BF16)
        qb_ref[0, h, 0, other:other + SB_HEAD_DIM, :] = zhalf
        vb_ref[0, h, 0] = vbT[h * SB_HEAD_DIM:(h + 1) * SB_HEAD_DIM].astype(BF16)

    ga_ref[0, 0] = pT[R_GA:R_QB]
    gb_ref[0, 0] = pT[R_GB:R_QM]

    qmT = pT[R_QM:R_GM]
    gmT = pT[R_GM:R_END]
    km = km_ref[0]
    vmT = vmT_ref[0]
    inv_sqrt_d = 1.0 / math.sqrt(MEM_HEAD_DIM)
    for h in range(MEM_HEADS):
        sl = slice(h * MEM_HEAD_DIM, (h + 1) * MEM_HEAD_DIM)
        s = jnp.dot(km[:, sl], qmT[sl].astype(BF16), preferred_element_type=F32) * inv_sqrt_d
        e = jnp.exp(s - jnp.max(s, axis=0, keepdims=True))
        inv_l = 1.0 / jnp.sum(e, axis=0, keepdims=True)
        o = jnp.dot(vmT[sl], e.astype(BF16), preferred_element_type=F32) * inv_l
        om_ref[0, 0, sl, :] = (o * _silu(gmT[sl])).astype(BF16)


def _mla_attn_kernel(q_ref, k_ref, v_ref, o_ref, sa_ref, sb_ref):
    n_heads, n_tiles = q_ref.shape[1], q_ref.shape[2]
    key_idx = lax.broadcasted_iota(jnp.int32, (TQ, TQ), 0)
    qry_idx = lax.broadcasted_iota(jnp.int32, (TQ, TQ), 1)
    causal = key_idx <= qry_idx

    def scores(g, qT, kj):
        start = pl.multiple_of(kj * TQ, TQ)
        k = k_ref[0, pl.ds(start, TQ), g * HEAD_PAD:(g + 1) * HEAD_PAD]
        return jnp.dot(k, qT, preferred_element_type=F32)

    def q_body(qi, carry):
        qTs = [q_ref[0, g, qi] for g in range(n_heads)]

        def fetch(dst_ref, kj):
            for g in range(n_heads):
                dst_ref[g] = scores(g, qTs[g], kj)

        def consume(src_ref, kj, state):
            new = []
            for g in range(n_heads):
                m, acc = state[g]
                s = src_ref[g]
                m_new = jnp.maximum(m, jnp.max(s, axis=0, keepdims=True))
                p = jnp.exp2(s - m_new)
                acc = jnp.exp2(m - m_new) * acc + jnp.dot(v_ref[0, g, kj], p.astype(BF16),
                                                          preferred_element_type=F32)
                new.append((m_new, acc))
            return tuple(new)

        for g in range(n_heads):
            sa_ref[g] = jnp.where(causal, scores(g, qTs[g], qi), NEG)
        state = tuple((jnp.full((1, TQ), NEG, F32), jnp.zeros((MLA_V_ROWS, TQ), F32))
                      for _ in range(n_heads))
        n_visit = qi + 1
        n_pairs = n_visit // 2

        def tile_of(i):
            return jnp.where(i == 0, qi, i - 1)

        def pair_body(p, state):
            fetch(sb_ref, tile_of(2 * p + 1))
            state = consume(sa_ref, tile_of(2 * p), state)
            fetch(sa_ref, tile_of(2 * p + 2))
            return consume(sb_ref, tile_of(2 * p + 1), state)

        state = lax.fori_loop(0, n_pairs, pair_body, state)
        state = lax.cond(n_visit % 2 == 1,
                         lambda st: consume(sa_ref, tile_of(2 * n_pairs), st),
                         lambda st: st, state)
        for g in range(n_heads):
            acc = state[g][1]
            o_ref[0, g, qi] = acc[0:MLA_V] * (1.0 / acc[MLA_V:MLA_V + 1])
        return carry

    lax.fori_loop(0, n_tiles, q_body, 0)


def _sb_attn_kernel(q_ref, k_ref, v_ref, tri_ref, o_ref):
    n_heads, n_tiles = q_ref.shape[1], q_ref.shape[2]
    key_idx = lax.broadcasted_iota(jnp.int32, (TQ, TQ), 0)
    qry_idx = lax.broadcasted_iota(jnp.int32, (TQ, TQ), 1)
    strict = key_idx < qry_idx

    def visit(qTs, kj, state, masked):
        start = pl.multiple_of(kj * TQ, TQ)
        us, suffixes, new = [], [], []
        for g in range(n_heads):
            blk = (g // 2) * HEAD_PAD
            us.append(jnp.dot(k_ref[0, pl.ds(start, TQ), blk:blk + HEAD_PAD], qTs[g],
                              preferred_element_type=F32))
        for g in range(n_heads):
            u = us[g]
            sp2 = jnp.maximum(u, 0.0) + jnp.log(1.0 + jnp.exp2(-jnp.abs(u))) * LOG2E
            if masked:
                sp2 = jnp.where(strict, sp2, 0.0)
            hi = sp2.astype(BF16)
            lo = (sp2 - hi.astype(F32)).astype(BF16)
            suffixes.append(jnp.dot(tri_ref[...], jnp.concatenate([hi, lo], axis=0),
                                    preferred_element_type=F32))
        for g in range(n_heads):
            run, acc = state[g]
            arg = (us[g] - run) - suffixes[g]
            if masked:
                arg = jnp.where(strict, arg, NEG)
            acc = acc + jnp.dot(v_ref[0, g, kj], jnp.exp2(arg).astype(BF16), preferred_element_type=F32)
            new.append((run + suffixes[g][0:1, :], acc))
        return tuple(new)

    def all_dead(state):
        run_min = state[0][0]
        for g in range(1, n_heads):
            run_min = jnp.minimum(run_min, state[g][0])
        return jnp.min(run_min) > SB_DEAD_LOG2

    def q_body(qi, carry):
        qTs = [q_ref[0, g, qi] for g in range(n_heads)]
        zero_run = jnp.zeros((1, TQ), F32)
        zero_acc = jnp.zeros((SB_HEAD_DIM, TQ), F32)
        state = visit(qTs, qi, tuple((zero_run, zero_acc) for _ in range(n_heads)), True)

        def cond(c):
            i, dead, _ = c
            return jnp.logical_and(i < qi, jnp.logical_not(dead))

        def body(c):
            i, _, state = c
            new = visit(qTs, qi - 1 - i, state, False)
            return i + 1, all_dead(new), new

        _, _, state = lax.while_loop(cond, body, (jnp.int32(0), all_dead(state), state))
        for g in range(n_heads):
            o_ref[0, g, qi] = state[g][1]
        return carry

    lax.fori_loop(0, n_tiles, q_body, 0)


def _out_block_kernel(x_ref, oa_ref, ob_ref, ga_ref, gb_ref, om_ref, waT_ref, wbT_ref, wmT_ref,
                      wgT_ref, bg_ref, wout_ref, lng_ref, lnb_ref, y_ref):
    x = x_ref[0]
    oa = oa_ref[0, :, 0].reshape(WIDTH, TQ)
    ob = ob_ref[0, :, 0].reshape(WIDTH, TQ)
    ha = (oa * _silu(ga_ref[0, 0])).astype(BF16)
    hb = (ob * _silu(gb_ref[0, 0])).astype(BF16)
    yaT = jnp.dot(waT_ref[...], ha, preferred_element_type=F32)
    ybT = jnp.dot(wbT_ref[...], hb, preferred_element_type=F32)
    ymT = jnp.dot(wmT_ref[...], om_ref[0, 0], preferred_element_type=F32)
    gT = _sigmoid(lax.dot_general(wgT_ref[...], x.astype(BF16), _NT, preferred_element_type=F32)
                  + bg_ref[...])
    merged = gT[0:D_MODEL] * yaT + gT[D_MODEL:2 * D_MODEL] * ybT + gT[2 * D_MODEL:] * ymT
    out = lax.dot_general(merged.astype(BF16), wout_ref[...], _TN, preferred_element_type=F32)
    r = DEEPNORM_ALPHA * x + out
    mu = jnp.mean(r, axis=1, keepdims=True)
    rc = r - mu
    var = jnp.mean(rc * rc, axis=1, keepdims=True)
    y_ref[0] = rc * lax.rsqrt(var + LN_EPS) * lng_ref[...] + lnb_ref[...]


def _const_spec(shape):
    return pl.BlockSpec(shape, lambda *_: (0,) * len(shape))


def _params(n_axes):
    return pltpu.CompilerParams(dimension_semantics=("parallel",) * n_axes, vmem_limit_bytes=VMEM_LIMIT)


def _layer(x, mem, w_in, w_mem_kv, q_a_gain, w_q_b, kv_a_gain, w_kv_b, w_branch_mla, w_branch_sb,
           w_branch_mem, w_merge_gate, b_merge_gate, w_out, ln_gain, ln_bias):
    B, S, D = x.shape
    assert D == D_MODEL and S % TQ == 0 and mem.shape == (B, MEM_LEN, D)
    NT = S // TQ
    half = MLA_ROPE // 2

    t_cols = np.concatenate([np.arange(O_CQ, O_KR), np.arange(O_GA, O_KB), np.arange(O_VB, O_END)])
    t_scale = np.ones((R_END, 1), np.float32)
    t_scale[R_QB:R_VB] = LOG2E / math.sqrt(SB_HEAD_DIM)
    wT = (w_in[:, t_cols].T * t_scale).astype(BF16)
    zeros96 = jnp.zeros((D, HEAD_PAD - MLA_ROPE), F32)
    w_rope = w_in[:, O_KR:O_GA]
    w_rot = jnp.concatenate([-w_rope[:, half:], w_rope[:, :half]], axis=1)
    wS = jnp.concatenate([w_in[:, O_CKV:O_KR], w_rope, zeros96, w_rot, zeros96, w_in[:, O_KB:O_VB]],
                         axis=1).astype(BF16)

    qd = MLA_NOPE + MLA_ROPE
    q_rows = np.concatenate(
        [np.concatenate([h * qd + np.arange(MLA_NOPE) for h in range(MLA_HEADS)]),
         np.concatenate([h * qd + MLA_NOPE + np.arange(half) for h in range(MLA_HEADS)]),
         np.concatenate([h * qd + MLA_NOPE + half + np.arange(half) for h in range(MLA_HEADS)])])
    wqT = w_q_b[:, q_rows].T.astype(BF16)
    kvd = MLA_NOPE + MLA_V
    v_rows = np.concatenate([h * kvd + MLA_NOPE + np.arange(MLA_V) for h in range(MLA_HEADS)])
    wvT = w_kv_b[:, v_rows].T.astype(BF16)
    knope_mask = np.zeros((1, MLA_HEADS * kvd), np.float32)
    place = np.zeros((HEAD_PAD, MLA_HEADS * HEAD_PAD), np.float32)
    for h in range(MLA_HEADS):
        knope_mask[0, h * kvd:h * kvd + MLA_NOPE] = 1.0
        place[np.arange(MLA_ROPE), h * HEAD_PAD + MLA_NOPE + np.arange(MLA_ROPE)] = 1.0
    wk = jnp.concatenate([w_kv_b * knope_mask, jnp.asarray(place)], axis=0).astype(BF16)

    qgain = q_a_gain.reshape(MLA_Q_LORA, 1)
    kvgain_col = kv_a_gain.reshape(MLA_KV_LORA, 1)
    kvgain_row = kv_a_gain.reshape(1, MLA_KV_LORA)

    freqs = ROPE_BASE ** (-jnp.arange(half, dtype=F32) / half)
    ang = jnp.arange(S, dtype=jnp.int32).astype(F32)[:, None] * freqs[None, :]
    cos, sin = jnp.cos(ang), jnp.sin(ang)
    cosq = jnp.tile(cos.T, (MLA_HEADS, 1))
    sinq = jnp.tile(sin.T, (MLA_HEADS, 1))
    zeros_k = jnp.zeros((S, HEAD_PAD - MLA_ROPE), F32)
    cosk = jnp.concatenate([cos, cos, zeros_k], axis=1)
    sink = jnp.concatenate([sin, sin, zeros_k], axis=1)

    wkm = w_mem_kv[:, :WIDTH].astype(BF16)
    wvmT = w_mem_kv[:, WIDTH:].T.astype(BF16)
    km, vmT = pl.pallas_call(
        _mem_kv_kernel,
        grid=(B,),
        in_specs=[pl.BlockSpec((1, MEM_LEN, D), lambda b: (b, 0, 0)),
                  _const_spec((D, WIDTH)), _const_spec((WIDTH, D))],
        out_specs=[pl.BlockSpec((1, MEM_LEN, WIDTH), lambda b: (b, 0, 0)),
                   pl.BlockSpec((1, WIDTH, MEM_LEN), lambda b: (b, 0, 0))],
        out_shape=[jax.ShapeDtypeStruct((B, MEM_LEN, WIDTH), BF16),
                   jax.ShapeDtypeStruct((B, WIDTH, MEM_LEN), BF16)],
        compiler_params=_params(1),
        name="mem_kv",
    )(mem, wkm, wvmT)

    head_q = lambda: pl.BlockSpec((1, MLA_HEADS, 1, HEAD_PAD, TQ), lambda b, t: (b, 0, t, 0, 0))
    head_v = lambda rows: pl.BlockSpec((1, MLA_HEADS, 1, rows, TQ), lambda b, t: (b, 0, t, 0, 0))
    wide = lambda: pl.BlockSpec((1, 1, WIDTH, TQ), lambda b, t: (b, t, 0, 0))
    qa, ka, va, qb, kb, vb, ga, gb, om = pl.pallas_call(
        _in_proj_kernel,
        grid=(B, NT),
        in_specs=[pl.BlockSpec((1, TQ, D), lambda b, t: (b, t, 0)),
                  _const_spec((R_END, D)), _const_spec((D, C_END)),
                  _const_spec((MLA_Q_LORA, 1)), _const_spec((MLA_KV_LORA, 1)), _const_spec((1, MLA_KV_LORA)),
                  _const_spec((MLA_HEADS * qd, MLA_Q_LORA)), _const_spec((WIDTH, MLA_KV_LORA)),
                  _const_spec((2 * HEAD_PAD, MLA_HEADS * HEAD_PAD)),
                  pl.BlockSpec((MLA_HEADS * half, TQ), lambda b, t: (0, t)),
                  pl.BlockSpec((MLA_HEADS * half, TQ), lambda b, t: (0, t)),
                  pl.BlockSpec((TQ, HEAD_PAD), lambda b, t: (t, 0)),
                  pl.BlockSpec((TQ, HEAD_PAD), lambda b, t: (t, 0)),
                  pl.BlockSpec((1, MEM_LEN, WIDTH), lambda b, t: (b, 0, 0)),
                  pl.BlockSpec((1, WIDTH, MEM_LEN), lambda b, t: (b, 0, 0))],
        out_specs=[head_q(),
                   pl.BlockSpec((1, TQ, MLA_HEADS * HEAD_PAD), lambda b, t: (b, t, 0)),
                   head_v(MLA_V_ROWS),
                   head_q(),
                   pl.BlockSpec((1, TQ, WIDTH), lambda b, t: (b, t, 0)),
                   head_v(SB_HEAD_DIM),
                   wide(), wide(), wide()],
        out_shape=[jax.ShapeDtypeStruct((B, MLA_HEADS, NT, HEAD_PAD, TQ), BF16),
                   jax.ShapeDtypeStruct((B, S, MLA_HEADS * HEAD_PAD), BF16),
                   jax.ShapeDtypeStruct((B, MLA_HEADS, NT, MLA_V_ROWS, TQ), BF16),
                   jax.ShapeDtypeStruct((B, SB_HEADS, NT, HEAD_PAD, TQ), BF16),
                   jax.ShapeDtypeStruct((B, S, WIDTH), BF16),
                   jax.ShapeDtypeStruct((B, SB_HEADS, NT, SB_HEAD_DIM, TQ), BF16),
                   jax.ShapeDtypeStruct((B, NT, WIDTH, TQ), F32),
                   jax.ShapeDtypeStruct((B, NT, WIDTH, TQ), F32),
                   jax.ShapeDtypeStruct((B, NT, WIDTH, TQ), BF16)],
        compiler_params=_params(2),
        name="in_proj",
    )(x, wT, wS, qgain, kvgain_col, kvgain_row, wqT, wvT, wk, cosq, sinq, cosk, sink, km, vmT)

    G = HEADS_PER_STEP
    group = lambda rows: pl.BlockSpec((1, G, NT, rows, TQ), lambda b, h: (b, h, 0, 0, 0))
    q_spec, v_spec = group(HEAD_PAD), group(MLA_V)
    o_shape = jax.ShapeDtypeStruct((B, MLA_HEADS, NT, MLA_V, TQ), F32)
    oa = pl.pallas_call(
        _mla_attn_kernel,
        grid=(B, MLA_HEADS // G),
        in_specs=[q_spec, pl.BlockSpec((1, S, G * HEAD_PAD), lambda b, h: (b, 0, h)), group(MLA_V_ROWS)],
        out_specs=v_spec,
        out_shape=o_shape,
        scratch_shapes=[pltpu.VMEM((G, TQ, TQ), F32), pltpu.VMEM((G, TQ, TQ), F32)],
        compiler_params=_params(2),
        name="mla_attn",
    )(qa, ka, va)

    tri_np = np.triu(np.ones((TQ, TQ), np.float32))
    tri = jnp.asarray(np.concatenate([tri_np, tri_np], axis=1), dtype=BF16)
    ob = pl.pallas_call(
        _sb_attn_kernel,
        grid=(B, SB_HEADS // G),
        in_specs=[q_spec, pl.BlockSpec((1, S, G * SB_HEAD_DIM), lambda b, h: (b, 0, h)), v_spec,
                  _const_spec((TQ, 2 * TQ))],
        out_specs=v_spec,
        out_shape=o_shape,
        compiler_params=_params(2),
        name="sb_attn",
    )(qb, kb, vb, tri)

    o_in = lambda: pl.BlockSpec((1, MLA_HEADS, 1, MLA_V, TQ), lambda b, t: (b, 0, t, 0, 0))
    y = pl.pallas_call(
        _out_block_kernel,
        grid=(B, NT),
        in_specs=[pl.BlockSpec((1, TQ, D), lambda b, t: (b, t, 0)),
                  o_in(), o_in(), wide(), wide(), wide(),
                  _const_spec((D, WIDTH)), _const_spec((D, WIDTH)), _const_spec((D, WIDTH)),
                  _const_spec((3 * D, D)), _const_spec((3 * D, 1)), _const_spec((D, D)),
                  _const_spec((1, D)), _const_spec((1, D))],
        out_specs=pl.BlockSpec((1, TQ, D), lambda b, t: (b, t, 0)),
        out_shape=jax.ShapeDtypeStruct((B, S, D), F32),
        compiler_params=_params(2),
        name="out_block",
    )(x, oa, ob, ga, gb, om,
      w_branch_mla.T.astype(BF16), w_branch_sb.T.astype(BF16), w_branch_mem.T.astype(BF16),
      w_merge_gate.T.astype(BF16), b_merge_gate.reshape(3 * D, 1), w_out.astype(BF16),
      ln_gain.reshape(1, D), ln_bias.reshape(1, D))
    return y


def kernel(x, mem, w_in, w_mem_kv, q_a_gain, w_q_b, kv_a_gain, w_kv_b, w_branch_mla, w_branch_sb,
           w_branch_mem, w_merge_gate, b_merge_gate, w_out, ln_gain, ln_bias):
    h = x
    for l in range(w_in.shape[0]):
        h = _layer(h, mem, w_in[l], w_mem_kv[l], q_a_gain[l], w_q_b[l], kv_a_gain[l], w_kv_b[l],
                   w_branch_mla[l], w_branch_sb[l], w_branch_mem[l], w_merge_gate[l], b_merge_gate[l],
                   w_out[l], ln_gain[l], ln_bias[l])
    return h
```

```python
import functools
import math

import numpy as np
import jax
import jax.numpy as jnp
from jax import lax
from jax.experimental import pallas as pl
from jax.experimental.pallas import tpu as pltpu

F32 = jnp.float32
BF16 = jnp.bfloat16

D_MODEL = 1024
MEM_LEN = 256
MLA_HEADS, MLA_NOPE, MLA_ROPE, MLA_V = 8, 64, 32, 64
MLA_Q_LORA, MLA_KV_LORA = 256, 128
SB_HEADS, SB_HEAD_DIM = 8, 64
MEM_HEADS, MEM_HEAD_DIM = 4, 128
WIDTH = 512
ROPE_BASE = 10000.0
RMS_EPS = 1e-6
LN_EPS = 1e-5
DEPTH = 1
DEEPNORM_ALPHA = (2.0 * DEPTH) ** 0.25

_OFF = np.cumsum([0, MLA_Q_LORA, MLA_KV_LORA, MLA_ROPE, WIDTH, WIDTH, WIDTH, WIDTH, WIDTH, WIDTH, WIDTH])
(O_CQ, O_CKV, O_KR, O_GA, O_QB, O_KB, O_VB, O_GB, O_QM, O_GM, O_END) = [int(v) for v in _OFF]

TQ = 256
HEAD_PAD = 128
VMEM_LIMIT = 56 * 1024 * 1024
NEG = -1e30
HEADS_PER_STEP = 4
DOT_ROWS = 512
MLA_V_ROWS = 80
LOG2E = math.log2(math.e)
SB_DEAD_LOG2 = 152.0

R_QM, R_CQ, R_CKV, R_GA, R_QB, R_VB, R_GB, R_GM, R_END = [
    int(v) for v in np.cumsum([0, WIDTH, MLA_Q_LORA, MLA_KV_LORA, WIDTH, WIDTH, WIDTH, WIDTH, WIDTH])]
C_CKV, C_G1, C_G2, C_KB, C_END = 0, 128, 256, 384, 896

_NT = (((1,), (1,)), ((), ()))
_TN = (((0,), (0,)), ((), ()))


def _sigmoid(t):
    return 1.0 / (1.0 + jnp.exp(-t))


def _silu(t):
    return t * _sigmoid(t)


def _mem_kv_kernel(mem_ref, wk_ref, wvT_ref, km_ref, vmT_ref):
    mb = mem_ref[0].astype(BF16)
    km_ref[0] = jnp.dot(mb, wk_ref[...], preferred_element_type=F32).astype(BF16)
    vmT_ref[0] = lax.dot_general(wvT_ref[...], mb, _NT, preferred_element_type=F32).astype(BF16)


def _in_proj_kernel(x_ref, wT_ref, wS_ref, qgain_ref, kvgain_col_ref, kvgain_row_ref, wqT_ref, wvT_ref,
                    wk_ref, cosq_ref, sinq_ref, cosk_ref, sink_ref, km_ref, vmT_ref,
                    qa_ref, ka_ref, va_ref, qb_ref, kb_ref, vb_ref, ga_ref, gb_ref, om_ref):
    xb = x_ref[0].astype(BF16)

    def proj_T(lo, hi):
        return lax.dot_general(wT_ref[lo:hi], xb, _NT, preferred_element_type=F32)

    half_m = WIDTH // 2
    qmT = [proj_T(R_QM, R_QM + half_m), proj_T(R_QM + half_m, R_CQ)]
    pA = proj_T(R_CQ, R_GA)
    gaT = proj_T(R_GA, R_QB)
    km = km_ref[0]
    vmT = vmT_ref[0]
    inv_sqrt_d = 1.0 / math.sqrt(MEM_HEAD_DIM)
    heads_m = [slice(h * MEM_HEAD_DIM, (h + 1) * MEM_HEAD_DIM) for h in range(MEM_HEADS)]
    sm = []
    for h, sl in enumerate(heads_m):
        q_h = qmT[h // 2][(h % 2) * MEM_HEAD_DIM:(h % 2 + 1) * MEM_HEAD_DIM].astype(BF16)
        sm.append(jnp.dot(km[:, sl], q_h, preferred_element_type=F32))
    pS = jnp.dot(xb, wS_ref[...], preferred_element_type=F32)
    qbT, vbT, gbT, gmT = (proj_T(lo, lo + WIDTH) for lo in (R_QB, R_VB, R_GB, R_GM))

    for h, sl in enumerate(heads_m):
        s = sm[h] * inv_sqrt_d
        e = jnp.exp(s - jnp.max(s, axis=0, keepdims=True))
        inv_l = 1.0 / jnp.sum(e, axis=0, keepdims=True)
        o = jnp.dot(vmT[sl], e.astype(BF16), preferred_element_type=F32) * inv_l
        om_ref[0, 0, sl, :] = (o * _silu(gmT[sl])).astype(BF16)

    cq = pA[0:MLA_Q_LORA]
    nq = cq * lax.rsqrt(jnp.mean(cq * cq, axis=0, keepdims=True) + RMS_EPS) * qgain_ref[...]
    qaT = jnp.dot(wqT_ref[...], nq.astype(BF16), preferred_element_type=F32)
    scale = LOG2E / math.sqrt(MLA_NOPE + MLA_ROPE)
    n_nope = MLA_HEADS * MLA_NOPE
    half = MLA_ROPE // 2
    x1 = qaT[n_nope:n_nope + MLA_HEADS * half]
    x2 = qaT[n_nope + MLA_HEADS * half:]
    cq_t, sq_t = cosq_ref[...], sinq_ref[...]
    r1 = (x1 * cq_t - x2 * sq_t) * scale
    r2 = (x1 * sq_t + x2 * cq_t) * scale
    nope = qaT[:n_nope] * scale
    zpad = jnp.zeros((HEAD_PAD - MLA_NOPE - MLA_ROPE, TQ), BF16)
    for h in range(MLA_HEADS):
        qa_ref[0, h, 0, 0:MLA_NOPE, :] = nope[h * MLA_NOPE:(h + 1) * MLA_NOPE].astype(BF16)
        qa_ref[0, h, 0, MLA_NOPE:MLA_NOPE + half, :] = r1[h * half:(h + 1) * half].astype(BF16)
        qa_ref[0, h, 0, MLA_NOPE + half:MLA_NOPE + MLA_ROPE, :] = r2[h * half:(h + 1) * half].astype(BF16)
        qa_ref[0, h, 0, MLA_NOPE + MLA_ROPE:, :] = zpad

    ckvT = pA[MLA_Q_LORA:]
    nkvT = ckvT * lax.rsqrt(jnp.mean(ckvT * ckvT, axis=0, keepdims=True) + RMS_EPS) * kvgain_col_ref[...]
    vaT = jnp.dot(wvT_ref[...], nkvT.astype(BF16), preferred_element_type=F32)
    row = lax.broadcasted_iota(jnp.int32, (MLA_V_ROWS - MLA_V, TQ), 0)
    ones_row = jnp.where(row == 0, 1.0, 0.0).astype(BF16)
    for h in range(MLA_HEADS):
        va_ref[0, h, 0, 0:MLA_V, :] = vaT[h * MLA_V:(h + 1) * MLA_V].astype(BF16)
        va_ref[0, h, 0, MLA_V:, :] = ones_row

    ckv = pS[:, C_CKV:C_G1]
    nkv = ckv * lax.rsqrt(jnp.mean(ckv * ckv, axis=1, keepdims=True) + RMS_EPS) * kvgain_row_ref[...]
    kpe = pS[:, C_G1:C_G2] * cosk_ref[...] + pS[:, C_G2:C_KB] * sink_ref[...]
    kin = jnp.concatenate([nkv.astype(BF16), kpe.astype(BF16)], axis=1)
    ka_ref[0] = jnp.dot(kin, wk_ref[...], preferred_element_type=F32).astype(BF16)

    kb_ref[0] = pS[:, C_KB:C_END].astype(BF16)
    zhalf = jnp.zeros((HEAD_PAD - SB_HEAD_DIM, TQ), BF16)
    for h in range(SB_HEADS):
        lo = (h % 2) * SB_HEAD_DIM
        other = SB_HEAD_DIM - lo
        qb_ref[0, h, 0, lo:lo + SB_HEAD_DIM, :] = qbT[h * SB_HEAD_DIM:(h + 1) * SB_HEAD_DIM].astype(BF16)
        qb_ref[0, h, 0, other:other + SB_HEAD_DIM, :] = zhalf
        vb_ref[0, h, 0] = vbT[h * SB_HEAD_DIM:(h + 1) * SB_HEAD_DIM].astype(BF16)

    ga_ref[0, 0] = gaT
    gb_ref[0, 0] = gbT


def _mla_attn_kernel(q_ref, k_ref, v_ref, o_ref, sa_ref, sb_ref):
    n_heads, n_tiles = q_ref.shape[1], q_ref.shape[2]
    key_idx = lax.broadcasted_iota(jnp.int32, (TQ, TQ), 0)
    qry_idx = lax.broadcasted_iota(jnp.int32, (TQ, TQ), 1)
    causal = key_idx <= qry_idx

    def scores(g, qT, kj):
        start = pl.multiple_of(kj * TQ, TQ)
        k = k_ref[0, pl.ds(start, TQ), g * HEAD_PAD:(g + 1) * HEAD_PAD]
        return jnp.dot(k, qT, preferred_element_type=F32)

    def q_body(qi, carry):
        qTs = [q_ref[0, g, qi] for g in range(n_heads)]

        def fetch(dst_ref, kj):
            for g in range(n_heads):
                dst_ref[g] = scores(g, qTs[g], kj)

        def consume(src_ref, kj, state):
            new = []
            for g in range(n_heads):
                m, acc = state[g]
                s = src_ref[g]
                m_new = jnp.maximum(m, jnp.max(s, axis=0, keepdims=True))
                p = jnp.exp2(s - m_new)
                acc = jnp.exp2(m - m_new) * acc + jnp.dot(v_ref[0, g, kj], p.astype(BF16),
                                                          preferred_element_type=F32)
                new.append((m_new, acc))
            return tuple(new)

        for g in range(n_heads):
            sa_ref[g] = jnp.where(causal, scores(g, qTs[g], qi), NEG)
        state = tuple((jnp.full((1, TQ), NEG, F32), jnp.zeros((MLA_V_ROWS, TQ), F32))
                      for _ in range(n_heads))
        n_visit = qi + 1
        n_pairs = n_visit // 2

        def tile_of(i):
            return jnp.where(i == 0, qi, i - 1)

        def pair_body(p, state):
            fetch(sb_ref, tile_of(2 * p + 1))
            state = consume(sa_ref, tile_of(2 * p), state)
            fetch(sa_ref, tile_of(2 * p + 2))
            return consume(sb_ref, tile_of(2 * p + 1), state)

        state = lax.fori_loop(0, n_pairs, pair_body, state)
        state = lax.cond(n_visit % 2 == 1,
                         lambda st: consume(sa_ref, tile_of(2 * n_pairs), st),
                         lambda st: st, state)
        for g in range(n_heads):
            acc = state[g][1]
            o_ref[0, g, qi] = acc[0:MLA_V] * (1.0 / acc[MLA_V:MLA_V + 1])
        return carry

    lax.fori_loop(0, n_tiles, q_body, 0)


def _sb_attn_kernel(q_ref, k_ref, v_ref, tri_ref, o_ref):
    n_heads, n_tiles = q_ref.shape[1], q_ref.shape[2]
    key_idx = lax.broadcasted_iota(jnp.int32, (TQ, TQ), 0)
    qry_idx = lax.broadcasted_iota(jnp.int32, (TQ, TQ), 1)
    strict = key_idx < qry_idx

    def visit(qTs, kj, state, masked):
        start = pl.multiple_of(kj * TQ, TQ)
        us, suffixes, new = [], [], []
        for g in range(n_heads):
            blk = (g // 2) * HEAD_PAD
            us.append(jnp.dot(k_ref[0, pl.ds(start, TQ), blk:blk + HEAD_PAD], qTs[g],
                              preferred_element_type=F32))
        for g in range(n_heads):
            u = us[g]
            sp2 = jnp.maximum(u, 0.0) + jnp.log(1.0 + jnp.exp2(-jnp.abs(u))) * LOG2E
            if masked:
                sp2 = jnp.where(strict, sp2, 0.0)
            hi = sp2.astype(BF16)
            lo = (sp2 - hi.astype(F32)).astype(BF16)
            suffixes.append(jnp.dot(tri_ref[...], jnp.concatenate([hi, lo], axis=0),
                                    preferred_element_type=F32))
        for g in range(n_heads):
            run, acc = state[g]
            arg = (us[g] - run) - suffixes[g]
            if masked:
                arg = jnp.where(strict, arg, NEG)
            acc = acc + jnp.dot(v_ref[0, g, kj], jnp.exp2(arg).astype(BF16), preferred_element_type=F32)
            new.append((run + suffixes[g][0:1, :], acc))
        return tuple(new)

    def all_dead(state):
        run_min = state[0][0]
        for g in range(1, n_heads):
            run_min = jnp.minimum(run_min, state[g][0])
        return jnp.min(run_min) > SB_DEAD_LOG2

    def q_body(qi, carry):
        qTs = [q_ref[0, g, qi] for g in range(n_heads)]
        zero_run = jnp.zeros((1, TQ), F32)
        zero_acc = jnp.zeros((SB_HEAD_DIM, TQ), F32)
        state = visit(qTs, qi, tuple((zero_run, zero_acc) for _ in range(n_heads)), True)

        def cond(c):
            i, dead, _ = c
            return jnp.logical_and(i < qi, jnp.logical_not(dead))

        def body(c):
            i, _, state = c
            new = visit(qTs, qi - 1 - i, state, False)
            return i + 1, all_dead(new), new

        _, _, state = lax.while_loop(cond, body, (jnp.int32(0), all_dead(state), state))
        for g in range(n_heads):
            o_ref[0, g, qi] = state[g][1]
        return carry

    lax.fori_loop(0, n_tiles, q_body, 0)


def _out_block_kernel(x_ref, oa_ref, ob_ref, ga_ref, gb_ref, om_ref, waT_ref, wbT_ref, wmT_ref,
                      wgT_ref, bg_ref, wout_ref, lng_ref, lnb_ref, y_ref):
    x = x_ref[0]
    xb = x.astype(BF16)
    chunks = [(lo, lo + DOT_ROWS) for lo in range(0, D_MODEL, DOT_ROWS)]
    oa = oa_ref[0, :, 0].reshape(WIDTH, TQ)
    ob = ob_ref[0, :, 0].reshape(WIDTH, TQ)
    hs = [(oa * _silu(ga_ref[0, 0])).astype(BF16), (ob * _silu(gb_ref[0, 0])).astype(BF16), om_ref[0, 0]]
    merged = []
    for lo, hi in chunks:
        acc = None
        for j, (w_ref, h) in enumerate(zip((waT_ref, wbT_ref, wmT_ref), hs)):
            rows = slice(j * D_MODEL + lo, j * D_MODEL + hi)
            z = lax.dot_general(wgT_ref[rows], xb, _NT, preferred_element_type=F32) + bg_ref[rows]
            term = _sigmoid(z) * jnp.dot(w_ref[lo:hi], h, preferred_element_type=F32)
            acc = term if acc is None else acc + term
        merged.append(acc.astype(BF16))
    merged = jnp.concatenate(merged, axis=0)
    out = lax.dot_general(merged, wout_ref[...], _TN, preferred_element_type=F32)
    r = DEEPNORM_ALPHA * x + out
    mu = jnp.mean(r, axis=1, keepdims=True)
    rc = r - mu
    var = jnp.mean(rc * rc, axis=1, keepdims=True)
    y_ref[0] = rc * lax.rsqrt(var + LN_EPS) * lng_ref[...] + lnb_ref[...]


def _const_spec(shape):
    return pl.BlockSpec(shape, lambda *_: (0,) * len(shape))


def _params(n_axes):
    return pltpu.CompilerParams(dimension_semantics=("parallel",) * n_axes, vmem_limit_bytes=VMEM_LIMIT)


def _layer(x, mem, w_in, w_mem_kv, q_a_gain, w_q_b, kv_a_gain, w_kv_b, w_branch_mla, w_branch_sb,
           w_branch_mem, w_merge_gate, b_merge_gate, w_out, ln_gain, ln_bias):
    B, S, D = x.shape
    assert D == D_MODEL and S % TQ == 0 and mem.shape == (B, MEM_LEN, D)
    NT = S // TQ
    half = MLA_ROPE // 2

    t_cols = np.concatenate([np.arange(O_QM, O_GM), np.arange(O_CQ, O_KR), np.arange(O_GA, O_KB),
                             np.arange(O_VB, O_QM), np.arange(O_GM, O_END)])
    t_scale = np.ones((R_END, 1), np.float32)
    t_scale[R_QB:R_VB] = LOG2E / math.sqrt(SB_HEAD_DIM)
    wT = (w_in[:, t_cols].T * t_scale).astype(BF16)
    zeros96 = jnp.zeros((D, HEAD_PAD - MLA_ROPE), F32)
    w_rope = w_in[:, O_KR:O_GA]
    w_rot = jnp.concatenate([-w_rope[:, half:], w_rope[:, :half]], axis=1)
    wS = jnp.concatenate([w_in[:, O_CKV:O_KR], w_rope, zeros96, w_rot, zeros96, w_in[:, O_KB:O_VB]],
                         axis=1).astype(BF16)

    qd = MLA_NOPE + MLA_ROPE
    q_rows = np.concatenate(
        [np.concatenate([h * qd + np.arange(MLA_NOPE) for h in range(MLA_HEADS)]),
         np.concatenate([h * qd + MLA_NOPE + np.arange(half) for h in range(MLA_HEADS)]),
         np.concatenate([h * qd + MLA_NOPE + half + np.arange(half) for h in range(MLA_HEADS)])])
    wqT = w_q_b[:, q_rows].T.astype(BF16)
    kvd = MLA_NOPE + MLA_V
    v_rows = np.concatenate([h * kvd + MLA_NOPE + np.arange(MLA_V) for h in range(MLA_HEADS)])
    wvT = w_kv_b[:, v_rows].T.astype(BF16)
    knope_mask = np.zeros((1, MLA_HEADS * kvd), np.float32)
    place = np.zeros((HEAD_PAD, MLA_HEADS * HEAD_PAD), np.float32)
    for h in range(MLA_HEADS):
        knope_mask[0, h * kvd:h * kvd + MLA_NOPE] = 1.0
        place[np.arange(MLA_ROPE), h * HEAD_PAD + MLA_NOPE + np.arange(MLA_ROPE)] = 1.0
    wk = jnp.concatenate([w_kv_b * knope_mask, jnp.asarray(place)], axis=0).astype(BF16)

    qgain = q_a_gain.reshape(MLA_Q_LORA, 1)
    kvgain_col = kv_a_gain.reshape(MLA_KV_LORA, 1)
    kvgain_row = kv_a_gain.reshape(1, MLA_KV_LORA)

    freqs = ROPE_BASE ** (-jnp.arange(half, dtype=F32) / half)
    ang = jnp.arange(S, dtype=jnp.int32).astype(F32)[:, None] * freqs[None, :]
    cos, sin = jnp.cos(ang), jnp.sin(ang)
    cosq = jnp.tile(cos.T, (MLA_HEADS, 1))
    sinq = jnp.tile(sin.T, (MLA_HEADS, 1))
    zeros_k = jnp.zeros((S, HEAD_PAD - MLA_ROPE), F32)
    cosk = jnp.concatenate([cos, cos, zeros_k], axis=1)
    sink = jnp.concatenate([sin, sin, zeros_k], axis=1)

    wkm = w_mem_kv[:, :WIDTH].astype(BF16)
    wvmT = w_mem_kv[:, WIDTH:].T.astype(BF16)
    km, vmT = pl.pallas_call(
        _mem_kv_kernel,
        grid=(B,),
        in_specs=[pl.BlockSpec((1, MEM_LEN, D), lambda b: (b, 0, 0)),
                  _const_spec((D, WIDTH)), _const_spec((WIDTH, D))],
        out_specs=[pl.BlockSpec((1, MEM_LEN, WIDTH), lambda b: (b, 0, 0)),
                   pl.BlockSpec((1, WIDTH, MEM_LEN), lambda b: (b, 0, 0))],
        out_shape=[jax.ShapeDtypeStruct((B, MEM_LEN, WIDTH), BF16),
                   jax.ShapeDtypeStruct((B, WIDTH, MEM_LEN), BF16)],
        compiler_params=_params(1),
        name="mem_kv",
    )(mem, wkm, wvmT)

    head_q = lambda: pl.BlockSpec((1, MLA_HEADS, 1, HEAD_PAD, TQ), lambda b, t: (b, 0, t, 0, 0))
    head_v = lambda rows: pl.BlockSpec((1, MLA_HEADS, 1, rows, TQ), lambda b, t: (b, 0, t, 0, 0))
    wide = lambda: pl.BlockSpec((1, 1, WIDTH, TQ), lambda b, t: (b, t, 0, 0))
    qa, ka, va, qb, kb, vb, ga, gb, om = pl.pallas_call(
        _in_proj_kernel,
        grid=(B, NT),
        in_specs=[pl.BlockSpec((1, TQ, D), lambda b, t: (b, t, 0)),
                  _const_spec((R_END, D)), _const_spec((D, C_END)),
                  _const_spec((MLA_Q_LORA, 1)), _const_spec((MLA_KV_LORA, 1)), _const_spec((1, MLA_KV_LORA)),
                  _const_spec((MLA_HEADS * qd, MLA_Q_LORA)), _const_spec((WIDTH, MLA_KV_LORA)),
                  _const_spec((2 * HEAD_PAD, MLA_HEADS * HEAD_PAD)),
                  pl.BlockSpec((MLA_HEADS * half, TQ), lambda b, t: (0, t)),
                  pl.BlockSpec((MLA_HEADS * half, TQ), lambda b, t: (0, t)),
                  pl.BlockSpec((TQ, HEAD_PAD), lambda b, t: (t, 0)),
                  pl.BlockSpec((TQ, HEAD_PAD), lambda b, t: (t, 0)),
                  pl.BlockSpec((1, MEM_LEN, WIDTH), lambda b, t: (b, 0, 0)),
                  pl.BlockSpec((1, WIDTH, MEM_LEN), lambda b, t: (b, 0, 0))],
        out_specs=[head_q(),
                   pl.BlockSpec((1, TQ, MLA_HEADS * HEAD_PAD), lambda b, t: (b, t, 0)),
                   head_v(MLA_V_ROWS),
                   head_q(),
                   pl.BlockSpec((1, TQ, WIDTH), lambda b, t: (b, t, 0)),
                   head_v(SB_HEAD_DIM),
                   wide(), wide(), wide()],
        out_shape=[jax.ShapeDtypeStruct((B, MLA_HEADS, NT, HEAD_PAD, TQ), BF16),
                   jax.ShapeDtypeStruct((B, S, MLA_HEADS * HEAD_PAD), BF16),
                   jax.ShapeDtypeStruct((B, MLA_HEADS, NT, MLA_V_ROWS, TQ), BF16),
                   jax.ShapeDtypeStruct((B, SB_HEADS, NT, HEAD_PAD, TQ), BF16),
                   jax.ShapeDtypeStruct((B, S, WIDTH), BF16),
                   jax.ShapeDtypeStruct((B, SB_HEADS, NT, SB_HEAD_DIM, TQ), BF16),
                   jax.ShapeDtypeStruct((B, NT, WIDTH, TQ), F32),
                   jax.ShapeDtypeStruct((B, NT, WIDTH, TQ), F32),
                   jax.ShapeDtypeStruct((B, NT, WIDTH, TQ), BF16)],
        compiler_params=_params(2),
        name="in_proj",
    )(x, wT, wS, qgain, kvgain_col, kvgain_row, wqT, wvT, wk, cosq, sinq, cosk, sink, km, vmT)

    G = HEADS_PER_STEP
    group = lambda rows: pl.BlockSpec((1, G, NT, rows, TQ), lambda b, h: (b, h, 0, 0, 0))
    q_spec, v_spec = group(HEAD_PAD), group(MLA_V)
    o_shape = jax.ShapeDtypeStruct((B, MLA_HEADS, NT, MLA_V, TQ), F32)
    oa = pl.pallas_call(
        _mla_attn_kernel,
        grid=(B, MLA_HEADS // G),
        in_specs=[q_spec, pl.BlockSpec((1, S, G * HEAD_PAD), lambda b, h: (b, 0, h)), group(MLA_V_ROWS)],
        out_specs=v_spec,
        out_shape=o_shape,
        scratch_shapes=[pltpu.VMEM((G, TQ, TQ), F32), pltpu.VMEM((G, TQ, TQ), F32)],
        compiler_params=_params(2),
        name="mla_attn",
    )(qa, ka, va)

    tri_np = np.triu(np.ones((TQ, TQ), np.float32))
    tri = jnp.asarray(np.concatenate([tri_np, tri_np], axis=1), dtype=BF16)
    ob = pl.pallas_call(
        _sb_attn_kernel,
        grid=(B, SB_HEADS // G),
        in_specs=[q_spec, pl.BlockSpec((1, S, G * SB_HEAD_DIM), lambda b, h: (b, 0, h)), v_spec,
                  _const_spec((TQ, 2 * TQ))],
        out_specs=v_spec,
        out_shape=o_shape,
        compiler_params=_params(2),
        name="sb_attn",
    )(qb, kb, vb, tri)

    o_in = lambda: pl.BlockSpec((1, MLA_HEADS, 1, MLA_V, TQ), lambda b, t: (b, 0, t, 0, 0))
    y = pl.pallas_call(
        _out_block_kernel,
        grid=(B, NT),
        in_specs=[pl.BlockSpec((1, TQ, D), lambda b, t: (b, t, 0)),
                  o_in(), o_in(), wide(), wide(), wide(),
                  _const_spec((D, WIDTH)), _const_spec((D, WIDTH)), _const_spec((D, WIDTH)),
                  _const_spec((3 * D, D)), _const_spec((3 * D, 1)), _const_spec((D, D)),
                  _const_spec((1, D)), _const_spec((1, D))],
        out_specs=pl.BlockSpec((1, TQ, D), lambda b, t: (b, t, 0)),
        out_shape=jax.ShapeDtypeStruct((B, S, D), F32),
        compiler_params=_params(2),
        name="out_block",
    )(x, oa, ob, ga, gb, om,
      w_branch_mla.T.astype(BF16), w_branch_sb.T.astype(BF16), w_branch_mem.T.astype(BF16),
      w_merge_gate.T.astype(BF16), b_merge_gate.reshape(3 * D, 1), w_out.astype(BF16),
      ln_gain.reshape(1, D), ln_bias.reshape(1, D))
    return y


def kernel(x, mem, w_in, w_mem_kv, q_a_gain, w_q_b, kv_a_gain, w_kv_b, w_branch_mla, w_branch_sb,
           w_branch_mem, w_merge_gate, b_merge_gate, w_out, ln_gain, ln_bias):
    h = x
    for l in range(w_in.shape[0]):
        h = _layer(h, mem, w_in[l], w_mem_kv[l], q_a_gain[l], w_q_b[l], kv_a_gain[l], w_kv_b[l],
                   w_branch_mla[l], w_branch_sb[l], w_branch_mem[l], w_merge_gate[l], b_merge_gate[l],
                   w_out[l], ln_gain[l], ln_bias[l])
    return h
```

```python
import functools
import math

import numpy as np
import jax
import jax.numpy as jnp
from jax import lax
from jax.experimental import pallas as pl
from jax.experimental.pallas import tpu as pltpu

F32 = jnp.float32
BF16 = jnp.bfloat16

D_MODEL = 1024
MEM_LEN = 256
MLA_HEADS, MLA_NOPE, MLA_ROPE, MLA_V = 8, 64, 32, 64
MLA_Q_LORA, MLA_KV_LORA = 256, 128
SB_HEADS, SB_HEAD_DIM = 8, 64
MEM_HEADS, MEM_HEAD_DIM = 4, 128
WIDTH = 512
ROPE_BASE = 10000.0
RMS_EPS = 1e-6
LN_EPS = 1e-5
DEPTH = 1
DEEPNORM_ALPHA = (2.0 * DEPTH) ** 0.25

_OFF = np.cumsum([0, MLA_Q_LORA, MLA_KV_LORA, MLA_ROPE, WIDTH, WIDTH, WIDTH, WIDTH, WIDTH, WIDTH, WIDTH])
(O_CQ, O_CKV, O_KR, O_GA, O_QB, O_KB, O_VB, O_GB, O_QM, O_GM, O_END) = [int(v) for v in _OFF]

TQ = 256
HEAD_PAD = 128
VMEM_LIMIT = 56 * 1024 * 1024
NEG = -1e30
HEADS_PER_STEP = 4
DOT_ROWS = 512
MLA_V_ROWS = 80
LOG2E = math.log2(math.e)
SB_DEAD_LOG2 = 152.0

R_QM, R_CQ, R_CKV, R_GA, R_QB, R_VB, R_GB, R_GM, R_END = [
    int(v) for v in np.cumsum([0, WIDTH, MLA_Q_LORA, MLA_KV_LORA, WIDTH, WIDTH, WIDTH, WIDTH, WIDTH])]
C_CKV, C_G1, C_G2, C_KB, C_END = 0, 128, 256, 384, 896

_NT = (((1,), (1,)), ((), ()))
_TN = (((0,), (0,)), ((), ()))


def _sigmoid(t):
    return 1.0 / (1.0 + jnp.exp(-t))


def _silu(t):
    return t * _sigmoid(t)


def _mem_kv_kernel(mem_ref, wk_ref, wvT_ref, km_ref, vmT_ref):
    mb = mem_ref[0].astype(BF16)
    km_ref[0] = jnp.dot(mb, wk_ref[...], preferred_element_type=F32).astype(BF16)
    vmT_ref[0] = lax.dot_general(wvT_ref[...], mb, _NT, preferred_element_type=F32).astype(BF16)


def _in_proj_kernel(x_ref, wT_ref, wS_ref, qgain_ref, kvgain_col_ref, kvgain_row_ref, wqT_ref, wvT_ref,
                    wk_ref, cosq_ref, sinq_ref, cosk_ref, sink_ref, km_ref, vmT_ref,
                    qa_ref, ka_ref, va_ref, qb_ref, kb_ref, vb_ref, ga_ref, gb_ref, om_ref):
    xb = x_ref[0].astype(BF16)

    def proj_T(lo, hi):
        return lax.dot_general(wT_ref[lo:hi], xb, _NT, preferred_element_type=F32)

    half_m = WIDTH // 2
    qmT = [proj_T(R_QM, R_QM + half_m), proj_T(R_QM + half_m, R_CQ)]
    pA = proj_T(R_CQ, R_GA)
    gaT = proj_T(R_GA, R_QB)
    km = km_ref[0]
    vmT = vmT_ref[0]
    inv_sqrt_d = 1.0 / math.sqrt(MEM_HEAD_DIM)
    heads_m = [slice(h * MEM_HEAD_DIM, (h + 1) * MEM_HEAD_DIM) for h in range(MEM_HEADS)]
    sm = []
    for h, sl in enumerate(heads_m):
        q_h = qmT[h // 2][(h % 2) * MEM_HEAD_DIM:(h % 2 + 1) * MEM_HEAD_DIM].astype(BF16)
        sm.append(jnp.dot(km[:, sl], q_h, preferred_element_type=F32))
    pS = jnp.dot(xb, wS_ref[...], preferred_element_type=F32)
    qbT, vbT, gbT, gmT = (proj_T(lo, lo + WIDTH) for lo in (R_QB, R_VB, R_GB, R_GM))

    for h, sl in enumerate(heads_m):
        s = sm[h] * inv_sqrt_d
        e = jnp.exp(s - jnp.max(s, axis=0, keepdims=True))
        inv_l = 1.0 / jnp.sum(e, axis=0, keepdims=True)
        o = jnp.dot(vmT[sl], e.astype(BF16), preferred_element_type=F32) * inv_l
        om_ref[0, 0, sl, :] = (o * _silu(gmT[sl])).astype(BF16)

    cq = pA[0:MLA_Q_LORA]
    nq = cq * lax.rsqrt(jnp.mean(cq * cq, axis=0, keepdims=True) + RMS_EPS) * qgain_ref[...]
    qaT = jnp.dot(wqT_ref[...], nq.astype(BF16), preferred_element_type=F32)
    scale = LOG2E / math.sqrt(MLA_NOPE + MLA_ROPE)
    n_nope = MLA_HEADS * MLA_NOPE
    half = MLA_ROPE // 2
    x1 = qaT[n_nope:n_nope + MLA_HEADS * half]
    x2 = qaT[n_nope + MLA_HEADS * half:]
    cq_t, sq_t = cosq_ref[...], sinq_ref[...]
    r1 = (x1 * cq_t - x2 * sq_t) * scale
    r2 = (x1 * sq_t + x2 * cq_t) * scale
    nope = qaT[:n_nope] * scale
    zpad = jnp.zeros((HEAD_PAD - MLA_NOPE - MLA_ROPE, TQ), BF16)
    for h in range(MLA_HEADS):
        qa_ref[0, h, 0, 0:MLA_NOPE, :] = nope[h * MLA_NOPE:(h + 1) * MLA_NOPE].astype(BF16)
        qa_ref[0, h, 0, MLA_NOPE:MLA_NOPE + half, :] = r1[h * half:(h + 1) * half].astype(BF16)
        qa_ref[0, h, 0, MLA_NOPE + half:MLA_NOPE + MLA_ROPE, :] = r2[h * half:(h + 1) * half].astype(BF16)
        qa_ref[0, h, 0, MLA_NOPE + MLA_ROPE:, :] = zpad

    ckvT = pA[MLA_Q_LORA:]
    nkvT = ckvT * lax.rsqrt(jnp.mean(ckvT * ckvT, axis=0, keepdims=True) + RMS_EPS) * kvgain_col_ref[...]
    vaT = jnp.dot(wvT_ref[...], nkvT.astype(BF16), preferred_element_type=F32)
    row = lax.broadcasted_iota(jnp.int32, (MLA_V_ROWS - MLA_V, TQ), 0)
    ones_row = jnp.where(row == 0, 1.0, 0.0).astype(BF16)
    for h in range(MLA_HEADS):
        va_ref[0, h, 0, 0:MLA_V, :] = vaT[h * MLA_V:(h + 1) * MLA_V].astype(BF16)
        va_ref[0, h, 0, MLA_V:, :] = ones_row

    ckv = pS[:, C_CKV:C_G1]
    nkv = ckv * lax.rsqrt(jnp.mean(ckv * ckv, axis=1, keepdims=True) + RMS_EPS) * kvgain_row_ref[...]
    kpe = pS[:, C_G1:C_G2] * cosk_ref[...] + pS[:, C_G2:C_KB] * sink_ref[...]
    kin = jnp.concatenate([nkv.astype(BF16), kpe.astype(BF16)], axis=1)
    ka_ref[0] = jnp.dot(kin, wk_ref[...], preferred_element_type=F32).astype(BF16)

    kb_ref[0] = pS[:, C_KB:C_END].astype(BF16)
    zhalf = jnp.zeros((HEAD_PAD - SB_HEAD_DIM, TQ), BF16)
    for h in range(SB_HEADS):
        lo = (h % 2) * SB_HEAD_DIM
        other = SB_HEAD_DIM - lo
        qb_ref[0, h, 0, lo:lo + SB_HEAD_DIM, :] = qbT[h * SB_HEAD_DIM:(h + 1) * SB_HEAD_DIM].astype(BF16)
        qb_ref[0, h, 0, other:other + SB_HEAD_DIM, :] = zhalf
        vb_ref[0, h, 0] = vbT[h * SB_HEAD_DIM:(h + 1) * SB_HEAD_DIM].astype(BF16)

    ga_ref[0, 0] = gaT
    gb_ref[0, 0] = gbT


def _mla_attn_kernel(q_ref, k_ref, v_ref, o_ref, sa_ref, sb_ref):
    n_heads, n_tiles = q_ref.shape[1], q_ref.shape[2]
    key_idx = lax.broadcasted_iota(jnp.int32, (TQ, TQ), 0)
    qry_idx = lax.broadcasted_iota(jnp.int32, (TQ, TQ), 1)
    causal = key_idx <= qry_idx

    def scores(g, qT, kj):
        start = pl.multiple_of(kj * TQ, TQ)
        k = k_ref[0, pl.ds(start, TQ), g * HEAD_PAD:(g + 1) * HEAD_PAD]
        return jnp.dot(k, qT, preferred_element_type=F32)

    def q_body(qi, carry):
        qTs = [q_ref[0, g, qi] for g in range(n_heads)]

        def fetch(dst_ref, kj):
            for g in range(n_heads):
                dst_ref[g] = scores(g, qTs[g], kj)

        def consume(src_ref, kj, state):
            new = []
            for g in range(n_heads):
                m, acc = state[g]
                s = src_ref[g]
                m_new = jnp.maximum(m, jnp.max(s, axis=0, keepdims=True))
                p = jnp.exp2(s - m_new)
                acc = jnp.exp2(m - m_new) * acc + jnp.dot(v_ref[0, g, kj], p.astype(BF16),
                                                          preferred_element_type=F32)
                new.append((m_new, acc))
            return tuple(new)

        for g in range(n_heads):
            sa_ref[g] = jnp.where(causal, scores(g, qTs[g], qi), NEG)
        state = tuple((jnp.full((1, TQ), NEG, F32), jnp.zeros((MLA_V_ROWS, TQ), F32))
                      for _ in range(n_heads))
        n_visit = qi + 1
        n_pairs = n_visit // 2

        def tile_of(i):
            return jnp.where(i == 0, qi, i - 1)

        def pair_body(p, state):
            fetch(sb_ref, tile_of(2 * p + 1))
            state = consume(sa_ref, tile_of(2 * p), state)
            fetch(sa_ref, tile_of(2 * p + 2))
            return consume(sb_ref, tile_of(2 * p + 1), state)

        state = lax.fori_loop(0, n_pairs, pair_body, state)
        state = lax.cond(n_visit % 2 == 1,
                         lambda st: consume(sa_ref, tile_of(2 * n_pairs), st),
                         lambda st: st, state)
        for g in range(n_heads):
            acc = state[g][1]
            o_ref[0, g, qi] = acc[0:MLA_V] * (1.0 / acc[MLA_V:MLA_V + 1])
        return carry

    lax.fori_loop(0, n_tiles, q_body, 0)


def _sb_attn_kernel(q_ref, k_ref, v_ref, tri_ref, o_ref):
    n_heads, n_tiles = q_ref.shape[1], q_ref.shape[2]
    key_idx = lax.broadcasted_iota(jnp.int32, (TQ, TQ), 0)
    qry_idx = lax.broadcasted_iota(jnp.int32, (TQ, TQ), 1)
    strict = key_idx < qry_idx

    def visit(chains, kjs, state, masked):
        starts = [pl.multiple_of(kj * TQ, TQ) for kj in kjs]
        us, suffixes, new = [], [], []
        for g, slot, qT in chains:
            blk = (g // 2) * HEAD_PAD
            us.append(jnp.dot(k_ref[0, pl.ds(starts[slot], TQ), blk:blk + HEAD_PAD], qT,
                              preferred_element_type=F32))
        for u in us:
            sp2 = jnp.maximum(u, 0.0) + jnp.log(1.0 + jnp.exp2(-jnp.abs(u))) * LOG2E
            if masked:
                sp2 = jnp.where(strict, sp2, 0.0)
            hi = sp2.astype(BF16)
            lo = (sp2 - hi.astype(F32)).astype(BF16)
            suffixes.append(jnp.dot(tri_ref[...], jnp.concatenate([hi, lo], axis=0),
                                    preferred_element_type=F32))
        for c, (g, slot, _) in enumerate(chains):
            run, acc = state[c]
            arg = (us[c] - run) - suffixes[c]
            if masked:
                arg = jnp.where(strict, arg, NEG)
            acc = acc + jnp.dot(v_ref[0, g, kjs[slot]], jnp.exp2(arg).astype(BF16),
                                preferred_element_type=F32)
            new.append((run + suffixes[c][0:1, :], acc))
        return tuple(new)

    def all_dead(state):
        run_min = state[0][0]
        for run, _ in state[1:]:
            run_min = jnp.minimum(run_min, run)
        return jnp.min(run_min) > SB_DEAD_LOG2

    def pair_body(j, carry):
        q_tiles = (2 * j, 2 * j + 1)
        chains = [(g, slot, q_ref[0, g, q_tiles[slot]]) for slot in range(2) for g in range(n_heads)]
        zero = (jnp.zeros((1, TQ), F32), jnp.zeros((SB_HEAD_DIM, TQ), F32))
        state = visit(chains, q_tiles, tuple(zero for _ in chains), True)

        def cond(c):
            i, dead, _ = c
            return jnp.logical_and(i <= q_tiles[1], jnp.logical_not(dead))

        def body(c):
            i, _, state = c
            spent = i > q_tiles[0]
            state = tuple((run if slot else jnp.where(spent, -NEG, run), acc)
                          for (_, slot, _), (run, acc) in zip(chains, state))
            new = visit(chains, (jnp.maximum(q_tiles[0] - i, 0), q_tiles[1] - i), state, False)
            return i + 1, all_dead(new), new

        _, _, state = lax.while_loop(cond, body, (jnp.int32(1), all_dead(state), state))
        for (g, slot, _), (_, acc) in zip(chains, state):
            o_ref[0, g, q_tiles[slot]] = acc
        return carry

    lax.fori_loop(0, n_tiles // 2, pair_body, 0)


def _out_block_kernel(x_ref, oa_ref, ob_ref, ga_ref, gb_ref, om_ref, wa_ref, wb_ref, wm_ref,
                      wg_ref, bg_ref, wout_ref, lng_ref, lnb_ref, y_ref):
    x = x_ref[0]
    xb = x.astype(BF16)
    oa = oa_ref[0, :, 0].reshape(WIDTH, TQ)
    ob = ob_ref[0, :, 0].reshape(WIDTH, TQ)
    hs = [(oa * _silu(ga_ref[0, 0])).astype(BF16), (ob * _silu(gb_ref[0, 0])).astype(BF16), om_ref[0, 0]]
    merged = None
    for j, (w_ref, h) in enumerate(zip((wa_ref, wb_ref, wm_ref), hs)):
        cols = slice(j * D_MODEL, (j + 1) * D_MODEL)
        z = jnp.dot(xb, wg_ref[:, cols], preferred_element_type=F32) + bg_ref[:, cols]
        term = _sigmoid(z) * lax.dot_general(h, w_ref[...], _TN, preferred_element_type=F32)
        merged = term if merged is None else merged + term
    out = jnp.dot(merged.astype(BF16), wout_ref[...], preferred_element_type=F32)
    r = DEEPNORM_ALPHA * x + out
    mu = jnp.mean(r, axis=1, keepdims=True)
    rc = r - mu
    var = jnp.mean(rc * rc, axis=1, keepdims=True)
    y_ref[0] = rc * lax.rsqrt(var + LN_EPS) * lng_ref[...] + lnb_ref[...]


def _const_spec(shape):
    return pl.BlockSpec(shape, lambda *_: (0,) * len(shape))


def _params(n_axes):
    return pltpu.CompilerParams(dimension_semantics=("parallel",) * n_axes, vmem_limit_bytes=VMEM_LIMIT)


def _layer(x, mem, w_in, w_mem_kv, q_a_gain, w_q_b, kv_a_gain, w_kv_b, w_branch_mla, w_branch_sb,
           w_branch_mem, w_merge_gate, b_merge_gate, w_out, ln_gain, ln_bias):
    B, S, D = x.shape
    assert D == D_MODEL and S % (2 * TQ) == 0 and mem.shape == (B, MEM_LEN, D)
    NT = S // TQ
    half = MLA_ROPE // 2

    sb_scale = LOG2E / math.sqrt(SB_HEAD_DIM)
    wT = jnp.concatenate([w_in[:, O_QM:O_GM], w_in[:, O_CQ:O_KR], w_in[:, O_GA:O_QB],
                          w_in[:, O_QB:O_KB] * sb_scale, w_in[:, O_VB:O_QM], w_in[:, O_GM:O_END]],
                         axis=1).astype(BF16).T
    zeros96 = jnp.zeros((D, HEAD_PAD - MLA_ROPE), F32)
    w_rope = w_in[:, O_KR:O_GA]
    w_rot = jnp.concatenate([-w_rope[:, half:], w_rope[:, :half]], axis=1)
    wS = jnp.concatenate([w_in[:, O_CKV:O_KR], w_rope, zeros96, w_rot, zeros96, w_in[:, O_KB:O_VB]],
                         axis=1).astype(BF16)

    qd = MLA_NOPE + MLA_ROPE
    wq3 = w_q_b.reshape(MLA_Q_LORA, MLA_HEADS, qd)
    wqT = jnp.concatenate([wq3[:, :, :MLA_NOPE].reshape(MLA_Q_LORA, -1),
                           wq3[:, :, MLA_NOPE:MLA_NOPE + half].reshape(MLA_Q_LORA, -1),
                           wq3[:, :, MLA_NOPE + half:].reshape(MLA_Q_LORA, -1)], axis=1).astype(BF16).T
    kvd = MLA_NOPE + MLA_V
    wvT = w_kv_b.reshape(MLA_KV_LORA, MLA_HEADS, kvd)[:, :, MLA_NOPE:].reshape(MLA_KV_LORA, -1).astype(BF16).T
    knope_mask = np.zeros((1, MLA_HEADS * kvd), np.float32)
    place = np.zeros((HEAD_PAD, MLA_HEADS * HEAD_PAD), np.float32)
    for h in range(MLA_HEADS):
        knope_mask[0, h * kvd:h * kvd + MLA_NOPE] = 1.0
        place[np.arange(MLA_ROPE), h * HEAD_PAD + MLA_NOPE + np.arange(MLA_ROPE)] = 1.0
    wk = jnp.concatenate([w_kv_b * knope_mask, jnp.asarray(place)], axis=0).astype(BF16)

    qgain = q_a_gain.reshape(MLA_Q_LORA, 1)
    kvgain_col = kv_a_gain.reshape(MLA_KV_LORA, 1)
    kvgain_row = kv_a_gain.reshape(1, MLA_KV_LORA)

    freqs = ROPE_BASE ** (-jnp.arange(half, dtype=F32) / half)
    ang = jnp.arange(S, dtype=jnp.int32).astype(F32)[:, None] * freqs[None, :]
    cos, sin = jnp.cos(ang), jnp.sin(ang)
    cosq = jnp.tile(cos.T, (MLA_HEADS, 1))
    sinq = jnp.tile(sin.T, (MLA_HEADS, 1))
    zeros_k = jnp.zeros((S, HEAD_PAD - MLA_ROPE), F32)
    cosk = jnp.concatenate([cos, cos, zeros_k], axis=1)
    sink = jnp.concatenate([sin, sin, zeros_k], axis=1)

    wkm = w_mem_kv[:, :WIDTH].astype(BF16)
    wvmT = w_mem_kv[:, WIDTH:].T.astype(BF16)
    km, vmT = pl.pallas_call(
        _mem_kv_kernel,
        grid=(B,),
        in_specs=[pl.BlockSpec((1, MEM_LEN, D), lambda b: (b, 0, 0)),
                  _const_spec((D, WIDTH)), _const_spec((WIDTH, D))],
        out_specs=[pl.BlockSpec((1, MEM_LEN, WIDTH), lambda b: (b, 0, 0)),
                   pl.BlockSpec((1, WIDTH, MEM_LEN), lambda b: (b, 0, 0))],
        out_shape=[jax.ShapeDtypeStruct((B, MEM_LEN, WIDTH), BF16),
                   jax.ShapeDtypeStruct((B, WIDTH, MEM_LEN), BF16)],
        compiler_params=_params(1),
        name="mem_kv",
    )(mem, wkm, wvmT)

    head_q = lambda: pl.BlockSpec((1, MLA_HEADS, 1, HEAD_PAD, TQ), lambda b, t: (b, 0, t, 0, 0))
    head_v = lambda rows: pl.BlockSpec((1, MLA_HEADS, 1, rows, TQ), lambda b, t: (b, 0, t, 0, 0))
    wide = lambda: pl.BlockSpec((1, 1, WIDTH, TQ), lambda b, t: (b, t, 0, 0))
    qa, ka, va, qb, kb, vb, ga, gb, om = pl.pallas_call(
        _in_proj_kernel,
        grid=(B, NT),
        in_specs=[pl.BlockSpec((1, TQ, D), lambda b, t: (b, t, 0)),
                  _const_spec((R_END, D)), _const_spec((D, C_END)),
                  _const_spec((MLA_Q_LORA, 1)), _const_spec((MLA_KV_LORA, 1)), _const_spec((1, MLA_KV_LORA)),
                  _const_spec((MLA_HEADS * qd, MLA_Q_LORA)), _const_spec((WIDTH, MLA_KV_LORA)),
                  _const_spec((2 * HEAD_PAD, MLA_HEADS * HEAD_PAD)),
                  pl.BlockSpec((MLA_HEADS * half, TQ), lambda b, t: (0, t)),
                  pl.BlockSpec((MLA_HEADS * half, TQ), lambda b, t: (0, t)),
                  pl.BlockSpec((TQ, HEAD_PAD), lambda b, t: (t, 0)),
                  pl.BlockSpec((TQ, HEAD_PAD), lambda b, t: (t, 0)),
                  pl.BlockSpec((1, MEM_LEN, WIDTH), lambda b, t: (b, 0, 0)),
                  pl.BlockSpec((1, WIDTH, MEM_LEN), lambda b, t: (b, 0, 0))],
        out_specs=[head_q(),
                   pl.BlockSpec((1, TQ, MLA_HEADS * HEAD_PAD), lambda b, t: (b, t, 0)),
                   head_v(MLA_V_ROWS),
                   head_q(),
                   pl.BlockSpec((1, TQ, WIDTH), lambda b, t: (b, t, 0)),
                   head_v(SB_HEAD_DIM),
                   wide(), wide(), wide()],
        out_shape=[jax.ShapeDtypeStruct((B, MLA_HEADS, NT, HEAD_PAD, TQ), BF16),
                   jax.ShapeDtypeStruct((B, S, MLA_HEADS * HEAD_PAD), BF16),
                   jax.ShapeDtypeStruct((B, MLA_HEADS, NT, MLA_V_ROWS, TQ), BF16),
                   jax.ShapeDtypeStruct((B, SB_HEADS, NT, HEAD_PAD, TQ), BF16),
                   jax.ShapeDtypeStruct((B, S, WIDTH), BF16),
                   jax.ShapeDtypeStruct((B, SB_HEADS, NT, SB_HEAD_DIM, TQ), BF16),
                   jax.ShapeDtypeStruct((B, NT, WIDTH, TQ), F32),
                   jax.ShapeDtypeStruct((B, NT, WIDTH, TQ), F32),
                   jax.ShapeDtypeStruct((B, NT, WIDTH, TQ), BF16)],
        compiler_params=_params(2),
        name="in_proj",
    )(x, wT, wS, qgain, kvgain_col, kvgain_row, wqT, wvT, wk, cosq, sinq, cosk, sink, km, vmT)

    G = HEADS_PER_STEP
    group = lambda rows: pl.BlockSpec((1, G, NT, rows, TQ), lambda b, h: (b, h, 0, 0, 0))
    q_spec, v_spec = group(HEAD_PAD), group(MLA_V)
    o_shape = jax.ShapeDtypeStruct((B, MLA_HEADS, NT, MLA_V, TQ), F32)
    oa = pl.pallas_call(
        _mla_attn_kernel,
        grid=(B, MLA_HEADS // G),
        in_specs=[q_spec, pl.BlockSpec((1, S, G * HEAD_PAD), lambda b, h: (b, 0, h)), group(MLA_V_ROWS)],
        out_specs=v_spec,
        out_shape=o_shape,
        scratch_shapes=[pltpu.VMEM((G, TQ, TQ), F32), pltpu.VMEM((G, TQ, TQ), F32)],
        compiler_params=_params(2),
        name="mla_attn",
    )(qa, ka, va)

    tri_np = np.triu(np.ones((TQ, TQ), np.float32))
    tri = jnp.asarray(np.concatenate([tri_np, tri_np], axis=1), dtype=BF16)
    ob = pl.pallas_call(
        _sb_attn_kernel,
        grid=(B, SB_HEADS // G),
        in_specs=[q_spec, pl.BlockSpec((1, S, G * SB_HEAD_DIM), lambda b, h: (b, 0, h)), v_spec,
                  _const_spec((TQ, 2 * TQ))],
        out_specs=v_spec,
        out_shape=o_shape,
        compiler_params=_params(2),
        name="sb_attn",
    )(qb, kb, vb, tri)

    o_in = lambda: pl.BlockSpec((1, MLA_HEADS, 1, MLA_V, TQ), lambda b, t: (b, 0, t, 0, 0))
    y = pl.pallas_call(
        _out_block_kernel,
        grid=(B, NT),
        in_specs=[pl.BlockSpec((1, TQ, D), lambda b, t: (b, t, 0)),
                  o_in(), o_in(), wide(), wide(), wide(),
                  _const_spec((WIDTH, D)), _const_spec((WIDTH, D)), _const_spec((WIDTH, D)),
                  _const_spec((D, 3 * D)), _const_spec((1, 3 * D)), _const_spec((D, D)),
                  _const_spec((1, D)), _const_spec((1, D))],
        out_specs=pl.BlockSpec((1, TQ, D), lambda b, t: (b, t, 0)),
        out_shape=jax.ShapeDtypeStruct((B, S, D), F32),
        compiler_params=_params(2),
        name="out_block",
    )(x, oa, ob, ga, gb, om,
      w_branch_mla.astype(BF16), w_branch_sb.astype(BF16), w_branch_mem.astype(BF16),
      w_merge_gate.astype(BF16), b_merge_gate.reshape(1, 3 * D), w_out.astype(BF16),
      ln_gain.reshape(1, D), ln_bias.reshape(1, D))
    return y


def kernel(x, mem, w_in, w_mem_kv, q_a_gain, w_q_b, kv_a_gain, w_kv_b, w_branch_mla, w_branch_sb,
           w_branch_mem, w_merge_gate, b_merge_gate, w_out, ln_gain, ln_bias):
    h = x
    for l in range(w_in.shape[0]):
        h = _layer(h, mem, w_in[l], w_mem_kv[l], q_a_gain[l], w_q_b[l], kv_a_gain[l], w_kv_b[l],
                   w_branch_mla[l], w_branch_sb[l], w_branch_mem[l], w_merge_gate[l], b_merge_gate[l],
                   w_out[l], ln_gain[l], ln_bias[l])
    return h
```

```python
import functools
import math

import numpy as np
import jax
import jax.numpy as jnp
from jax import lax
from jax.experimental import pallas as pl
from jax.experimental.pallas import tpu as pltpu

F32 = jnp.float32
BF16 = jnp.bfloat16

D_MODEL = 1024
MEM_LEN = 256
MLA_HEADS, MLA_NOPE, MLA_ROPE, MLA_V = 8, 64, 32, 64
MLA_Q_LORA, MLA_KV_LORA = 256, 128
SB_HEADS, SB_HEAD_DIM = 8, 64
MEM_HEADS, MEM_HEAD_DIM = 4, 128
WIDTH = 512
ROPE_BASE = 10000.0
RMS_EPS = 1e-6
LN_EPS = 1e-5
DEPTH = 1
DEEPNORM_ALPHA = (2.0 * DEPTH) ** 0.25

_OFF = np.cumsum([0, MLA_Q_LORA, MLA_KV_LORA, MLA_ROPE, WIDTH, WIDTH, WIDTH, WIDTH, WIDTH, WIDTH, WIDTH])
(O_CQ, O_CKV, O_KR, O_GA, O_QB, O_KB, O_VB, O_GB, O_QM, O_GM, O_END) = [int(v) for v in _OFF]

TQ = 256
HEAD_PAD = 128
VMEM_LIMIT = 56 * 1024 * 1024
NEG = -1e30
HEADS_PER_STEP = 4
OUT_TILES = 1
MLA_V_ROWS = 80
LOG2E = math.log2(math.e)
SB_DEAD_LOG2 = 152.0

R_QM, R_CQ, R_CKV, R_GA, R_QB, R_VB, R_GB, R_GM, R_END = [
    int(v) for v in np.cumsum([0, WIDTH, MLA_Q_LORA, MLA_KV_LORA, WIDTH, WIDTH, WIDTH, WIDTH, WIDTH])]
C_CKV, C_G1, C_G2, C_KB, C_END = 0, 128, 256, 384, 896

_NT = (((1,), (1,)), ((), ()))
_TN = (((0,), (0,)), ((), ()))


def _sigmoid(t):
    return 1.0 / (1.0 + jnp.exp(-t))


def _silu(t):
    return t * _sigmoid(t)


def _mem_kv_kernel(mem_ref, wk_ref, wvT_ref, km_ref, vmT_ref):
    mb = mem_ref[0].astype(BF16)
    km_ref[0] = jnp.dot(mb, wk_ref[...], preferred_element_type=F32).astype(BF16)
    vmT_ref[0] = lax.dot_general(wvT_ref[...], mb, _NT, preferred_element_type=F32).astype(BF16)


def _in_proj_kernel(x_ref, wT_ref, wS_ref, qgain_ref, kvgain_col_ref, kvgain_row_ref, wqT_ref, wvT_ref,
                    wk_ref, cosq_ref, sinq_ref, cosk_ref, sink_ref, km_ref, vmT_ref,
                    qa_ref, ka_ref, va_ref, qb_ref, kb_ref, vb_ref, ga_ref, gb_ref, om_ref):
    xb = x_ref[0].astype(BF16)

    def proj_T(lo, hi):
        return lax.dot_general(wT_ref[lo:hi], xb, _NT, preferred_element_type=F32)

    half_m = WIDTH // 2
    qmT = [proj_T(R_QM, R_QM + half_m), proj_T(R_QM + half_m, R_CQ)]
    pA = proj_T(R_CQ, R_GA)
    gaT = proj_T(R_GA, R_QB)
    km = km_ref[0]
    vmT = vmT_ref[0]
    inv_sqrt_d = 1.0 / math.sqrt(MEM_HEAD_DIM)
    heads_m = [slice(h * MEM_HEAD_DIM, (h + 1) * MEM_HEAD_DIM) for h in range(MEM_HEADS)]
    sm = []
    for h, sl in enumerate(heads_m):
        q_h = qmT[h // 2][(h % 2) * MEM_HEAD_DIM:(h % 2 + 1) * MEM_HEAD_DIM].astype(BF16)
        sm.append(jnp.dot(km[:, sl], q_h, preferred_element_type=F32))
    pS = jnp.dot(xb, wS_ref[...], preferred_element_type=F32)
    qbT, vbT, gbT, gmT = (proj_T(lo, lo + WIDTH) for lo in (R_QB, R_VB, R_GB, R_GM))

    for h, sl in enumerate(heads_m):
        s = sm[h] * inv_sqrt_d
        e = jnp.exp(s - jnp.max(s, axis=0, keepdims=True))
        inv_l = 1.0 / jnp.sum(e, axis=0, keepdims=True)
        o = jnp.dot(vmT[sl], e.astype(BF16), preferred_element_type=F32) * inv_l
        om_ref[0, 0, sl, :] = (o * _silu(gmT[sl])).astype(BF16)

    cq = pA[0:MLA_Q_LORA]
    nq = cq * lax.rsqrt(jnp.mean(cq * cq, axis=0, keepdims=True) + RMS_EPS) * qgain_ref[...]
    qaT = jnp.dot(wqT_ref[...], nq.astype(BF16), preferred_element_type=F32)
    scale = LOG2E / math.sqrt(MLA_NOPE + MLA_ROPE)
    n_nope = MLA_HEADS * MLA_NOPE
    half = MLA_ROPE // 2
    x1 = qaT[n_nope:n_nope + MLA_HEADS * half]
    x2 = qaT[n_nope + MLA_HEADS * half:]
    cq_t, sq_t = cosq_ref[...], sinq_ref[...]
    r1 = (x1 * cq_t - x2 * sq_t) * scale
    r2 = (x1 * sq_t + x2 * cq_t) * scale
    nope = qaT[:n_nope] * scale
    zpad = jnp.zeros((HEAD_PAD - MLA_NOPE - MLA_ROPE, TQ), BF16)
    for h in range(MLA_HEADS):
        qa_ref[0, h, 0, 0:MLA_NOPE, :] = nope[h * MLA_NOPE:(h + 1) * MLA_NOPE].astype(BF16)
        qa_ref[0, h, 0, MLA_NOPE:MLA_NOPE + half, :] = r1[h * half:(h + 1) * half].astype(BF16)
        qa_ref[0, h, 0, MLA_NOPE + half:MLA_NOPE + MLA_ROPE, :] = r2[h * half:(h + 1) * half].astype(BF16)
        qa_ref[0, h, 0, MLA_NOPE + MLA_ROPE:, :] = zpad

    ckvT = pA[MLA_Q_LORA:]
    nkvT = ckvT * lax.rsqrt(jnp.mean(ckvT * ckvT, axis=0, keepdims=True) + RMS_EPS) * kvgain_col_ref[...]
    vaT = jnp.dot(wvT_ref[...], nkvT.astype(BF16), preferred_element_type=F32)
    row = lax.broadcasted_iota(jnp.int32, (MLA_V_ROWS - MLA_V, TQ), 0)
    ones_row = jnp.where(row == 0, 1.0, 0.0).astype(BF16)
    for h in range(MLA_HEADS):
        va_ref[0, h, 0, 0:MLA_V, :] = vaT[h * MLA_V:(h + 1) * MLA_V].astype(BF16)
        va_ref[0, h, 0, MLA_V:, :] = ones_row

    ckv = pS[:, C_CKV:C_G1]
    nkv = ckv * lax.rsqrt(jnp.mean(ckv * ckv, axis=1, keepdims=True) + RMS_EPS) * kvgain_row_ref[...]
    kpe = pS[:, C_G1:C_G2] * cosk_ref[...] + pS[:, C_G2:C_KB] * sink_ref[...]
    kin = jnp.concatenate([nkv.astype(BF16), kpe.astype(BF16)], axis=1)
    ka_ref[0] = jnp.dot(kin, wk_ref[...], preferred_element_type=F32).astype(BF16)

    kb_ref[0] = pS[:, C_KB:C_END].astype(BF16)
    zhalf = jnp.zeros((HEAD_PAD - SB_HEAD_DIM, TQ), BF16)
    for h in range(SB_HEADS):
        lo = (h % 2) * SB_HEAD_DIM
        other = SB_HEAD_DIM - lo
        qb_ref[0, h, 0, lo:lo + SB_HEAD_DIM, :] = qbT[h * SB_HEAD_DIM:(h + 1) * SB_HEAD_DIM].astype(BF16)
        qb_ref[0, h, 0, other:other + SB_HEAD_DIM, :] = zhalf
        vb_ref[0, h, 0] = vbT[h * SB_HEAD_DIM:(h + 1) * SB_HEAD_DIM].astype(BF16)

    ga_ref[0, 0] = gaT
    gb_ref[0, 0] = gbT


def _mla_attn_kernel(q_ref, k_ref, v_ref, o_ref, sa_ref, sb_ref, acc_ref):
    n_heads, n_tiles = q_ref.shape[1], q_ref.shape[2]
    key_idx = lax.broadcasted_iota(jnp.int32, (TQ, TQ), 0)
    qry_idx = lax.broadcasted_iota(jnp.int32, (TQ, TQ), 1)
    causal = key_idx <= qry_idx

    def scores(g, qT, kj):
        start = pl.multiple_of(kj * TQ, TQ)
        k = k_ref[0, pl.ds(start, TQ), g * HEAD_PAD:(g + 1) * HEAD_PAD]
        return jnp.dot(k, qT, preferred_element_type=F32)

    def q_body(qi, carry):
        qTs = [q_ref[0, g, qi] for g in range(n_heads)]

        def fetch(dst_ref, kj):
            for g in range(n_heads):
                dst_ref[g] = scores(g, qTs[g], kj)

        def consume(src_ref, kj, state):
            new = []
            for g in range(n_heads):
                m = state[g]
                s = src_ref[g]
                m_new = jnp.maximum(m, jnp.max(s, axis=0, keepdims=True))
                p = jnp.exp2(s - m_new)
                acc_ref[g] = jnp.exp2(m - m_new) * acc_ref[g] + jnp.dot(
                    v_ref[0, g, kj], p.astype(BF16), preferred_element_type=F32)
                new.append(m_new)
            return tuple(new)

        for g in range(n_heads):
            sa_ref[g] = jnp.where(causal, scores(g, qTs[g], qi), NEG)
        acc_ref[...] = jnp.zeros(acc_ref.shape, F32)
        state = tuple(jnp.full((1, TQ), NEG, F32) for _ in range(n_heads))
        n_visit = qi + 1
        n_pairs = n_visit // 2
        n_quads = n_visit // 4

        def tile_of(i):
            return jnp.where(i == 0, qi, i - 1)

        def pair_body(p, state):
            fetch(sb_ref, tile_of(2 * p + 1))
            state = consume(sa_ref, tile_of(2 * p), state)
            fetch(sa_ref, tile_of(2 * p + 2))
            return consume(sb_ref, tile_of(2 * p + 1), state)

        def quad_body(q, state):
            return pair_body(2 * q + 1, pair_body(2 * q, state))

        state = lax.fori_loop(0, n_quads, quad_body, state)
        state = lax.fori_loop(2 * n_quads, n_pairs, pair_body, state)
        state = lax.cond(n_visit % 2 == 1,
                         lambda st: consume(sa_ref, tile_of(2 * n_pairs), st),
                         lambda st: st, state)
        for g in range(n_heads):
            acc = acc_ref[g]
            o_ref[0, g, qi] = acc[0:MLA_V] * (1.0 / acc[MLA_V:MLA_V + 1])
        return carry

    lax.fori_loop(0, n_tiles, q_body, 0)


def _sb_attn_kernel(q_ref, k_ref, v_ref, tri_ref, o_ref):
    n_heads, n_tiles = q_ref.shape[1], q_ref.shape[2]
    key_idx = lax.broadcasted_iota(jnp.int32, (TQ, TQ), 0)
    qry_idx = lax.broadcasted_iota(jnp.int32, (TQ, TQ), 1)
    strict = key_idx < qry_idx

    def visit(chains, q_tiles, kjs, runs, masked):
        starts = [pl.multiple_of(kj * TQ, TQ) for kj in kjs]
        us, suffixes, new = [], [], []
        for g, slot, qT in chains:
            blk = (g // 2) * HEAD_PAD
            us.append(jnp.dot(k_ref[0, pl.ds(starts[slot], TQ), blk:blk + HEAD_PAD], qT,
                              preferred_element_type=F32))
        for u in us:
            sp2 = jnp.maximum(u, 0.0) + jnp.log(1.0 + jnp.exp2(-jnp.abs(u))) * LOG2E
            if masked:
                sp2 = jnp.where(strict, sp2, 0.0)
            hi = sp2.astype(BF16)
            lo = (sp2 - hi.astype(F32)).astype(BF16)
            suffixes.append(jnp.dot(tri_ref[...], jnp.concatenate([hi, lo], axis=0),
                                    preferred_element_type=F32))
        for c, (g, slot, _) in enumerate(chains):
            arg = (us[c] - runs[c]) - suffixes[c]
            if masked:
                arg = jnp.where(strict, arg, NEG)
            pv = jnp.dot(v_ref[0, g, kjs[slot]], jnp.exp2(arg).astype(BF16), preferred_element_type=F32)
            o_ref[0, g, q_tiles[slot]] = pv if masked else o_ref[0, g, q_tiles[slot]] + pv
            new.append(runs[c] + suffixes[c][0:1, :])
        return tuple(new)

    def all_dead(runs):
        run_min = runs[0]
        for run in runs[1:]:
            run_min = jnp.minimum(run_min, run)
        return jnp.min(run_min) > SB_DEAD_LOG2

    def pair_body(j, carry):
        q_tiles = (2 * j, 2 * j + 1)
        chains = [(g, slot, q_ref[0, g, q_tiles[slot]]) for slot in range(2) for g in range(n_heads)]
        runs = visit(chains, q_tiles, q_tiles, tuple(jnp.zeros((1, TQ), F32) for _ in chains), True)

        def cond(c):
            i, dead, _ = c
            return jnp.logical_and(i <= q_tiles[1], jnp.logical_not(dead))

        def body(c):
            i, _, runs = c
            spent = i > q_tiles[0]
            runs = tuple(run if slot else jnp.where(spent, -NEG, run)
                         for (_, slot, _), run in zip(chains, runs))
            new = visit(chains, q_tiles, (jnp.maximum(q_tiles[0] - i, 0), q_tiles[1] - i), runs, False)
            return i + 1, all_dead(new), new

        lax.while_loop(cond, body, (jnp.int32(1), all_dead(runs), runs))
        return carry

    lax.fori_loop(0, n_tiles // 2, pair_body, 0)


def _out_block_kernel(x_ref, oa_ref, ob_ref, ga_ref, gb_ref, om_ref, wa_ref, wb_ref, wm_ref,
                      wg_ref, bg_ref, wout_ref, lng_ref, lnb_ref, y_ref):
    x = x_ref[0]
    xb = x.astype(BF16)
    hs = [[], [], []]
    for t in range(OUT_TILES):
        oa = oa_ref[0, :, t].reshape(WIDTH, TQ)
        ob = ob_ref[0, :, t].reshape(WIDTH, TQ)
        hs[0].append((oa * _silu(ga_ref[0, t])).astype(BF16))
        hs[1].append((ob * _silu(gb_ref[0, t])).astype(BF16))
        hs[2].append(om_ref[0, t])
    merged = None
    for j, w_ref in enumerate((wa_ref, wb_ref, wm_ref)):
        cols = slice(j * D_MODEL, (j + 1) * D_MODEL)
        z = jnp.dot(xb, wg_ref[:, cols], preferred_element_type=F32) + bg_ref[:, cols]
        y = jnp.concatenate([lax.dot_general(h, w_ref[...], _TN, preferred_element_type=F32)
                             for h in hs[j]], axis=0)
        term = _sigmoid(z) * y
        merged = term if merged is None else merged + term
    out = jnp.dot(merged.astype(BF16), wout_ref[...], preferred_element_type=F32)
    r = DEEPNORM_ALPHA * x + out
    mu = jnp.mean(r, axis=1, keepdims=True)
    rc = r - mu
    var = jnp.mean(rc * rc, axis=1, keepdims=True)
    y_ref[0] = rc * lax.rsqrt(var + LN_EPS) * lng_ref[...] + lnb_ref[...]


def _const_spec(shape):
    return pl.BlockSpec(shape, lambda *_: (0,) * len(shape))


def _params(n_axes):
    return pltpu.CompilerParams(dimension_semantics=("parallel",) * n_axes, vmem_limit_bytes=VMEM_LIMIT)


def _layer(x, mem, w_in, w_mem_kv, q_a_gain, w_q_b, kv_a_gain, w_kv_b, w_branch_mla, w_branch_sb,
           w_branch_mem, w_merge_gate, b_merge_gate, w_out, ln_gain, ln_bias):
    B, S, D = x.shape
    assert D == D_MODEL and S % (2 * TQ) == 0 and mem.shape == (B, MEM_LEN, D)
    NT = S // TQ
    half = MLA_ROPE // 2

    sb_scale = LOG2E / math.sqrt(SB_HEAD_DIM)
    wT = jnp.concatenate([w_in[:, O_QM:O_GM], w_in[:, O_CQ:O_KR], w_in[:, O_GA:O_QB],
                          w_in[:, O_QB:O_KB] * sb_scale, w_in[:, O_VB:O_QM], w_in[:, O_GM:O_END]],
                         axis=1).astype(BF16).T
    zeros96 = jnp.zeros((D, HEAD_PAD - MLA_ROPE), F32)
    w_rope = w_in[:, O_KR:O_GA]
    w_rot = jnp.concatenate([-w_rope[:, half:], w_rope[:, :half]], axis=1)
    wS = jnp.concatenate([w_in[:, O_CKV:O_KR], w_rope, zeros96, w_rot, zeros96, w_in[:, O_KB:O_VB]],
                         axis=1).astype(BF16)

    qd = MLA_NOPE + MLA_ROPE
    wq3 = w_q_b.reshape(MLA_Q_LORA, MLA_HEADS, qd)
    wqT = jnp.concatenate([wq3[:, :, :MLA_NOPE].reshape(MLA_Q_LORA, -1),
                           wq3[:, :, MLA_NOPE:MLA_NOPE + half].reshape(MLA_Q_LORA, -1),
                           wq3[:, :, MLA_NOPE + half:].reshape(MLA_Q_LORA, -1)], axis=1).astype(BF16).T
    kvd = MLA_NOPE + MLA_V
    wvT = w_kv_b.reshape(MLA_KV_LORA, MLA_HEADS, kvd)[:, :, MLA_NOPE:].reshape(MLA_KV_LORA, -1).astype(BF16).T
    knope_mask = np.zeros((1, MLA_HEADS * kvd), np.float32)
    place = np.zeros((HEAD_PAD, MLA_HEADS * HEAD_PAD), np.float32)
    for h in range(MLA_HEADS):
        knope_mask[0, h * kvd:h * kvd + MLA_NOPE] = 1.0
        place[np.arange(MLA_ROPE), h * HEAD_PAD + MLA_NOPE + np.arange(MLA_ROPE)] = 1.0
    wk = jnp.concatenate([w_kv_b * knope_mask, jnp.asarray(place)], axis=0).astype(BF16)

    qgain = q_a_gain.reshape(MLA_Q_LORA, 1)
    kvgain_col = kv_a_gain.reshape(MLA_KV_LORA, 1)
    kvgain_row = kv_a_gain.reshape(1, MLA_KV_LORA)

    freqs = ROPE_BASE ** (-jnp.arange(half, dtype=F32) / half)
    ang = jnp.arange(S, dtype=jnp.int32).astype(F32)[:, None] * freqs[None, :]
    cos, sin = jnp.cos(ang), jnp.sin(ang)
    cosq = jnp.tile(cos.T, (MLA_HEADS, 1))
    sinq = jnp.tile(sin.T, (MLA_HEADS, 1))
    zeros_k = jnp.zeros((S, HEAD_PAD - MLA_ROPE), F32)
    cosk = jnp.concatenate([cos, cos, zeros_k], axis=1)
    sink = jnp.concatenate([sin, sin, zeros_k], axis=1)

    wkm = w_mem_kv[:, :WIDTH].astype(BF16)
    wvmT = w_mem_kv[:, WIDTH:].T.astype(BF16)
    km, vmT = pl.pallas_call(
        _mem_kv_kernel,
        grid=(B,),
        in_specs=[pl.BlockSpec((1, MEM_LEN, D), lambda b: (b, 0, 0)),
                  _const_spec((D, WIDTH)), _const_spec((WIDTH, D))],
        out_specs=[pl.BlockSpec((1, MEM_LEN, WIDTH), lambda b: (b, 0, 0)),
                   pl.BlockSpec((1, WIDTH, MEM_LEN), lambda b: (b, 0, 0))],
        out_shape=[jax.ShapeDtypeStruct((B, MEM_LEN, WIDTH), BF16),
                   jax.ShapeDtypeStruct((B, WIDTH, MEM_LEN), BF16)],
        compiler_params=_params(1),
        name="mem_kv",
    )(mem, wkm, wvmT)

    head_q = lambda: pl.BlockSpec((1, MLA_HEADS, 1, HEAD_PAD, TQ), lambda b, t: (b, 0, t, 0, 0))
    head_v = lambda rows: pl.BlockSpec((1, MLA_HEADS, 1, rows, TQ), lambda b, t: (b, 0, t, 0, 0))
    wide = lambda: pl.BlockSpec((1, 1, WIDTH, TQ), lambda b, t: (b, t, 0, 0))
    qa, ka, va, qb, kb, vb, ga, gb, om = pl.pallas_call(
        _in_proj_kernel,
        grid=(B, NT),
        in_specs=[pl.BlockSpec((1, TQ, D), lambda b, t: (b, t, 0)),
                  _const_spec((R_END, D)), _const_spec((D, C_END)),
                  _const_spec((MLA_Q_LORA, 1)), _const_spec((MLA_KV_LORA, 1)), _const_spec((1, MLA_KV_LORA)),
                  _const_spec((MLA_HEADS * qd, MLA_Q_LORA)), _const_spec((WIDTH, MLA_KV_LORA)),
                  _const_spec((2 * HEAD_PAD, MLA_HEADS * HEAD_PAD)),
                  pl.BlockSpec((MLA_HEADS * half, TQ), lambda b, t: (0, t)),
                  pl.BlockSpec((MLA_HEADS * half, TQ), lambda b, t: (0, t)),
                  pl.BlockSpec((TQ, HEAD_PAD), lambda b, t: (t, 0)),
                  pl.BlockSpec((TQ, HEAD_PAD), lambda b, t: (t, 0)),
                  pl.BlockSpec((1, MEM_LEN, WIDTH), lambda b, t: (b, 0, 0)),
                  pl.BlockSpec((1, WIDTH, MEM_LEN), lambda b, t: (b, 0, 0))],
        out_specs=[head_q(),
                   pl.BlockSpec((1, TQ, MLA_HEADS * HEAD_PAD), lambda b, t: (b, t, 0)),
                   head_v(MLA_V_ROWS),
                   head_q(),
                   pl.BlockSpec((1, TQ, WIDTH), lambda b, t: (b, t, 0)),
                   head_v(SB_HEAD_DIM),
                   wide(), wide(), wide()],
        out_shape=[jax.ShapeDtypeStruct((B, MLA_HEADS, NT, HEAD_PAD, TQ), BF16),
                   jax.ShapeDtypeStruct((B, S, MLA_HEADS * HEAD_PAD), BF16),
                   jax.ShapeDtypeStruct((B, MLA_HEADS, NT, MLA_V_ROWS, TQ), BF16),
                   jax.ShapeDtypeStruct((B, SB_HEADS, NT, HEAD_PAD, TQ), BF16),
                   jax.ShapeDtypeStruct((B, S, WIDTH), BF16),
                   jax.ShapeDtypeStruct((B, SB_HEADS, NT, SB_HEAD_DIM, TQ), BF16),
                   jax.ShapeDtypeStruct((B, NT, WIDTH, TQ), F32),
                   jax.ShapeDtypeStruct((B, NT, WIDTH, TQ), F32),
                   jax.ShapeDtypeStruct((B, NT, WIDTH, TQ), BF16)],
        compiler_params=_params(2),
        name="in_proj",
    )(x, wT, wS, qgain, kvgain_col, kvgain_row, wqT, wvT, wk, cosq, sinq, cosk, sink, km, vmT)

    G = HEADS_PER_STEP
    group = lambda rows: pl.BlockSpec((1, G, NT, rows, TQ), lambda b, h: (b, h, 0, 0, 0))
    q_spec, v_spec = group(HEAD_PAD), group(MLA_V)
    o_shape = jax.ShapeDtypeStruct((B, MLA_HEADS, NT, MLA_V, TQ), F32)
    oa = pl.pallas_call(
        _mla_attn_kernel,
        grid=(B, MLA_HEADS // G),
        in_specs=[q_spec, pl.BlockSpec((1, S, G * HEAD_PAD), lambda b, h: (b, 0, h)), group(MLA_V_ROWS)],
        out_specs=v_spec,
        out_shape=o_shape,
        scratch_shapes=[pltpu.VMEM((G, TQ, TQ), F32), pltpu.VMEM((G, TQ, TQ), F32),
                        pltpu.VMEM((G, MLA_V_ROWS, TQ), F32)],
        compiler_params=_params(2),
        name="mla_attn",
    )(qa, ka, va)

    tri_np = np.triu(np.ones((TQ, TQ), np.float32))
    tri = jnp.asarray(np.concatenate([tri_np, tri_np], axis=1), dtype=BF16)
    ob = pl.pallas_call(
        _sb_attn_kernel,
        grid=(B, SB_HEADS // G),
        in_specs=[q_spec, pl.BlockSpec((1, S, G * SB_HEAD_DIM), lambda b, h: (b, 0, h)), v_spec,
                  _const_spec((TQ, 2 * TQ))],
        out_specs=v_spec,
        out_shape=o_shape,
        compiler_params=_params(2),
        name="sb_attn",
    )(qb, kb, vb, tri)

    TO = OUT_TILES
    o_in = lambda: pl.BlockSpec((1, MLA_HEADS, TO, MLA_V, TQ), lambda b, t: (b, 0, t, 0, 0))
    wide_o = lambda: pl.BlockSpec((1, TO, WIDTH, TQ), lambda b, t: (b, t, 0, 0))
    y = pl.pallas_call(
        _out_block_kernel,
        grid=(B, NT // TO),
        in_specs=[pl.BlockSpec((1, TO * TQ, D), lambda b, t: (b, t, 0)),
                  o_in(), o_in(), wide_o(), wide_o(), wide_o(),
                  _const_spec((WIDTH, D)), _const_spec((WIDTH, D)), _const_spec((WIDTH, D)),
                  _const_spec((D, 3 * D)), _const_spec((1, 3 * D)), _const_spec((D, D)),
                  _const_spec((1, D)), _const_spec((1, D))],
        out_specs=pl.BlockSpec((1, TO * TQ, D), lambda b, t: (b, t, 0)),
        out_shape=jax.ShapeDtypeStruct((B, S, D), F32),
        compiler_params=_params(2),
        name="out_block",
    )(x, oa, ob, ga, gb, om,
      w_branch_mla.astype(BF16), w_branch_sb.astype(BF16), w_branch_mem.astype(BF16),
      w_merge_gate.astype(BF16), b_merge_gate.reshape(1, 3 * D), w_out.astype(BF16),
      ln_gain.reshape(1, D), ln_bias.reshape(1, D))
    return y


def kernel(x, mem, w_in, w_mem_kv, q_a_gain, w_q_b, kv_a_gain, w_kv_b, w_branch_mla, w_branch_sb,
           w_branch_mem, w_merge_gate, b_merge_gate, w_out, ln_gain, ln_bias):
    h = x
    for l in range(w_in.shape[0]):
        h = _layer(h, mem, w_in[l], w_mem_kv[l], q_a_gain[l], w_q_b[l], kv_a_gain[l], w_kv_b[l],
                   w_branch_mla[l], w_branch_sb[l], w_branch_mem[l], w_merge_gate[l], b_merge_gate[l],
                   w_out[l], ln_gain[l], ln_bias[l])
    return h
```

```python
import functools
import math

import numpy as np
import jax
import jax.numpy as jnp
from jax import lax
from jax.experimental import pallas as pl
from jax.experimental.pallas import tpu as pltpu

F32 = jnp.float32
BF16 = jnp.bfloat16

D_MODEL = 1024
MEM_LEN = 256
MLA_HEADS, MLA_NOPE, MLA_ROPE, MLA_V = 8, 64, 32, 64
MLA_Q_LORA, MLA_KV_LORA = 256, 128
SB_HEADS, SB_HEAD_DIM = 8, 64
MEM_HEADS, MEM_HEAD_DIM = 4, 128
WIDTH = 512
ROPE_BASE = 10000.0
RMS_EPS = 1e-6
LN_EPS = 1e-5
DEPTH = 1
DEEPNORM_ALPHA = (2.0 * DEPTH) ** 0.25

_OFF = np.cumsum([0, MLA_Q_LORA, MLA_KV_LORA, MLA_ROPE, WIDTH, WIDTH, WIDTH, WIDTH, WIDTH, WIDTH, WIDTH])
(O_CQ, O_CKV, O_KR, O_GA, O_QB, O_KB, O_VB, O_GB, O_QM, O_GM, O_END) = [int(v) for v in _OFF]

TQ = 256
HEAD_PAD = 128
VMEM_LIMIT = 56 * 1024 * 1024
NEG = -1e30
HEADS_PER_STEP = 4
OUT_TILES = 1
MLA_V_ROWS = 80
LOG2E = math.log2(math.e)
SB_DEAD_LOG2 = 152.0

R_QM, R_CQ, R_CKV, R_GA, R_QB, R_VB, R_GB, R_GM, R_END = [
    int(v) for v in np.cumsum([0, WIDTH, MLA_Q_LORA, MLA_KV_LORA, WIDTH, WIDTH, WIDTH, WIDTH, WIDTH])]
C_CKV, C_G1, C_G2, C_KB, C_END = 0, 128, 256, 384, 896

_NT = (((1,), (1,)), ((), ()))
_TN = (((0,), (0,)), ((), ()))


def _sigmoid(t):
    return 1.0 / (1.0 + jnp.exp(-t))


def _silu(t):
    return t * _sigmoid(t)


def _mem_kv_kernel(mem_ref, wk_ref, wvT_ref, km_ref, vmT_ref):
    mb = mem_ref[0].astype(BF16)
    km_ref[0] = jnp.dot(mb, wk_ref[...], preferred_element_type=F32).astype(BF16)
    vmT_ref[0] = lax.dot_general(wvT_ref[...], mb, _NT, preferred_element_type=F32).astype(BF16)


def _in_proj_kernel(x_ref, wT_ref, wS_ref, qgain_ref, kvgain_col_ref, kvgain_row_ref, wqT_ref, wvT_ref,
                    wk_ref, cosq_ref, sinq_ref, cosk_ref, sink_ref, km_ref, vmT_ref,
                    qa_ref, ka_ref, va_ref, qb_ref, kb_ref, vb_ref, ga_ref, gb_ref, om_ref):
    xb = x_ref[0].astype(BF16)

    def proj_T(lo, hi):
        return lax.dot_general(wT_ref[lo:hi], xb, _NT, preferred_element_type=F32)

    half_m = WIDTH // 2
    qmT = [proj_T(R_QM, R_QM + half_m), proj_T(R_QM + half_m, R_CQ)]
    pA = proj_T(R_CQ, R_GA)
    gaT = proj_T(R_GA, R_QB)
    km = km_ref[0]
    vmT = vmT_ref[0]
    inv_sqrt_d = 1.0 / math.sqrt(MEM_HEAD_DIM)
    heads_m = [slice(h * MEM_HEAD_DIM, (h + 1) * MEM_HEAD_DIM) for h in range(MEM_HEADS)]
    sm = []
    for h, sl in enumerate(heads_m):
        q_h = qmT[h // 2][(h % 2) * MEM_HEAD_DIM:(h % 2 + 1) * MEM_HEAD_DIM].astype(BF16)
        sm.append(jnp.dot(km[:, sl], q_h, preferred_element_type=F32))
    pS = jnp.dot(xb, wS_ref[...], preferred_element_type=F32)
    qbT, vbT, gbT, gmT = (proj_T(lo, lo + WIDTH) for lo in (R_QB, R_VB, R_GB, R_GM))

    for h, sl in enumerate(heads_m):
        s = sm[h] * inv_sqrt_d
        e = jnp.exp(s - jnp.max(s, axis=0, keepdims=True))
        inv_l = 1.0 / jnp.sum(e, axis=0, keepdims=True)
        o = jnp.dot(vmT[sl], e.astype(BF16), preferred_element_type=F32) * inv_l
        om_ref[0, 0, sl, :] = (o * _silu(gmT[sl])).astype(BF16)

    cq = pA[0:MLA_Q_LORA]
    nq = cq * lax.rsqrt(jnp.mean(cq * cq, axis=0, keepdims=True) + RMS_EPS) * qgain_ref[...]
    qaT = jnp.dot(wqT_ref[...], nq.astype(BF16), preferred_element_type=F32)
    scale = LOG2E / math.sqrt(MLA_NOPE + MLA_ROPE)
    n_nope = MLA_HEADS * MLA_NOPE
    half = MLA_ROPE // 2
    x1 = qaT[n_nope:n_nope + MLA_HEADS * half]
    x2 = qaT[n_nope + MLA_HEADS * half:]
    cq_t, sq_t = cosq_ref[...], sinq_ref[...]
    r1 = (x1 * cq_t - x2 * sq_t) * scale
    r2 = (x1 * sq_t + x2 * cq_t) * scale
    nope = qaT[:n_nope] * scale
    zpad = jnp.zeros((HEAD_PAD - MLA_NOPE - MLA_ROPE, TQ), BF16)
    for h in range(MLA_HEADS):
        qa_ref[0, h, 0, 0:MLA_NOPE, :] = nope[h * MLA_NOPE:(h + 1) * MLA_NOPE].astype(BF16)
        qa_ref[0, h, 0, MLA_NOPE:MLA_NOPE + half, :] = r1[h * half:(h + 1) * half].astype(BF16)
        qa_ref[0, h, 0, MLA_NOPE + half:MLA_NOPE + MLA_ROPE, :] = r2[h * half:(h + 1) * half].astype(BF16)
        qa_ref[0, h, 0, MLA_NOPE + MLA_ROPE:, :] = zpad

    ckvT = pA[MLA_Q_LORA:]
    nkvT = ckvT * lax.rsqrt(jnp.mean(ckvT * ckvT, axis=0, keepdims=True) + RMS_EPS) * kvgain_col_ref[...]
    vaT = jnp.dot(wvT_ref[...], nkvT.astype(BF16), preferred_element_type=F32)
    row = lax.broadcasted_iota(jnp.int32, (MLA_V_ROWS - MLA_V, TQ), 0)
    ones_row = jnp.where(row == 0, 1.0, 0.0).astype(BF16)
    for h in range(MLA_HEADS):
        va_ref[0, h, 0, 0:MLA_V, :] = vaT[h * MLA_V:(h + 1) * MLA_V].astype(BF16)
        va_ref[0, h, 0, MLA_V:, :] = ones_row

    ckv = pS[:, C_CKV:C_G1]
    nkv = ckv * lax.rsqrt(jnp.mean(ckv * ckv, axis=1, keepdims=True) + RMS_EPS) * kvgain_row_ref[...]
    kpe = pS[:, C_G1:C_G2] * cosk_ref[...] + pS[:, C_G2:C_KB] * sink_ref[...]
    kin = jnp.concatenate([nkv.astype(BF16), kpe.astype(BF16)], axis=1)
    ka_ref[0] = jnp.dot(kin, wk_ref[...], preferred_element_type=F32).astype(BF16)

    kb_ref[0] = pS[:, C_KB:C_END].astype(BF16)
    zhalf = jnp.zeros((HEAD_PAD - SB_HEAD_DIM, TQ), BF16)
    for h in range(SB_HEADS):
        lo = (h % 2) * SB_HEAD_DIM
        other = SB_HEAD_DIM - lo
        qb_ref[0, h, 0, lo:lo + SB_HEAD_DIM, :] = qbT[h * SB_HEAD_DIM:(h + 1) * SB_HEAD_DIM].astype(BF16)
        qb_ref[0, h, 0, other:other + SB_HEAD_DIM, :] = zhalf
        vb_ref[0, h, 0] = vbT[h * SB_HEAD_DIM:(h + 1) * SB_HEAD_DIM].astype(BF16)

    ga_ref[0, 0] = gaT
    gb_ref[0, 0] = gbT


def _mla_attn_kernel(tab_ref, q_ref, k_ref, v_ref, o_ref, sa_ref, sb_ref, m_ref, acc_ref):
    n_heads, n_tiles = q_ref.shape[1], q_ref.shape[2]
    key_idx = lax.broadcasted_iota(jnp.int32, (TQ, TQ), 0)
    qry_idx = lax.broadcasted_iota(jnp.int32, (TQ, TQ), 1)
    causal = key_idx <= qry_idx

    def sweep(row, n_visits, diagonal):
        def fetch(dst_ref, i):
            qi, kj = tab_ref[row, i], tab_ref[row + 1, i]
            start = pl.multiple_of(kj * TQ, TQ)
            for g in range(n_heads):
                s = jnp.dot(k_ref[0, pl.ds(start, TQ), g * HEAD_PAD:(g + 1) * HEAD_PAD], q_ref[0, g, qi],
                            preferred_element_type=F32)
                dst_ref[g] = jnp.where(causal, s, NEG) if diagonal else s

        def consume(src_ref, i):
            qi, kj = tab_ref[row, i], tab_ref[row + 1, i]
            for g in range(n_heads):
                slot = qi * n_heads + g
                s = src_ref[g]
                m_new = jnp.max(s, axis=0, keepdims=True)
                if not diagonal:
                    m_old = m_ref[slot]
                    m_new = jnp.maximum(m_old, m_new)
                pv = jnp.dot(v_ref[0, g, kj], jnp.exp2(s - m_new).astype(BF16),
                             preferred_element_type=F32)
                acc_ref[slot] = pv if diagonal else jnp.exp2(m_old - m_new) * acc_ref[slot] + pv
                m_ref[slot] = m_new

        def pair(p):
            fetch(sb_ref, 2 * p + 1)
            consume(sa_ref, 2 * p)
            fetch(sa_ref, 2 * p + 2)
            consume(sb_ref, 2 * p + 1)

        def quad_body(q, carry):
            pair(2 * q)
            pair(2 * q + 1)
            return carry

        fetch(sa_ref, 0)
        lax.fori_loop(0, n_visits // 4, quad_body, 0)

    sweep(0, n_tiles, True)
    sweep(2, n_tiles * (n_tiles - 1) // 2, False)

    def finish(qi, carry):
        for g in range(n_heads):
            acc = acc_ref[qi * n_heads + g]
            o_ref[0, g, qi] = acc[0:MLA_V] * (1.0 / acc[MLA_V:MLA_V + 1])
        return carry

    lax.fori_loop(0, n_tiles, finish, 0)


def _sb_attn_kernel(q_ref, k_ref, v_ref, tri_ref, o_ref):
    n_heads, n_tiles = q_ref.shape[1], q_ref.shape[2]
    key_idx = lax.broadcasted_iota(jnp.int32, (TQ, TQ), 0)
    qry_idx = lax.broadcasted_iota(jnp.int32, (TQ, TQ), 1)
    strict = key_idx < qry_idx

    def visit(chains, q_tiles, kjs, runs, masked):
        starts = [pl.multiple_of(kj * TQ, TQ) for kj in kjs]
        us, suffixes, new = [], [], []
        for g, slot, qT in chains:
            blk = (g // 2) * HEAD_PAD
            us.append(jnp.dot(k_ref[0, pl.ds(starts[slot], TQ), blk:blk + HEAD_PAD], qT,
                              preferred_element_type=F32))
        for u in us:
            sp2 = jnp.maximum(u, 0.0) + jnp.log(1.0 + jnp.exp2(-jnp.abs(u))) * LOG2E
            if masked:
                sp2 = jnp.where(strict, sp2, 0.0)
            hi = sp2.astype(BF16)
            lo = (sp2 - hi.astype(F32)).astype(BF16)
            suffixes.append(jnp.dot(tri_ref[...], jnp.concatenate([hi, lo], axis=0),
                                    preferred_element_type=F32))
        for c, (g, slot, _) in enumerate(chains):
            arg = (us[c] - runs[c]) - suffixes[c]
            if masked:
                arg = jnp.where(strict, arg, NEG)
            pv = jnp.dot(v_ref[0, g, kjs[slot]], jnp.exp2(arg).astype(BF16), preferred_element_type=F32)
            o_ref[0, g, q_tiles[slot]] = pv if masked else o_ref[0, g, q_tiles[slot]] + pv
            new.append(runs[c] + suffixes[c][0:1, :])
        return tuple(new)

    def all_dead(runs):
        run_min = runs[0]
        for run in runs[1:]:
            run_min = jnp.minimum(run_min, run)
        return jnp.min(run_min) > SB_DEAD_LOG2

    def pair_body(j, carry):
        q_tiles = (2 * j, 2 * j + 1)
        chains = [(g, slot, q_ref[0, g, q_tiles[slot]]) for slot in range(2) for g in range(n_heads)]
        runs = visit(chains, q_tiles, q_tiles, tuple(jnp.zeros((1, TQ), F32) for _ in chains), True)

        def cond(c):
            i, dead, _ = c
            return jnp.logical_and(i <= q_tiles[1], jnp.logical_not(dead))

        def body(c):
            i, _, runs = c
            spent = i > q_tiles[0]
            runs = tuple(run if slot else jnp.where(spent, -NEG, run)
                         for (_, slot, _), run in zip(chains, runs))
            new = visit(chains, q_tiles, (jnp.maximum(q_tiles[0] - i, 0), q_tiles[1] - i), runs, False)
            return i + 1, all_dead(new), new

        lax.while_loop(cond, body, (jnp.int32(1), all_dead(runs), runs))
        return carry

    lax.fori_loop(0, n_tiles // 2, pair_body, 0)


def _out_block_kernel(x_ref, oa_ref, ob_ref, ga_ref, gb_ref, om_ref, wa_ref, wb_ref, wm_ref,
                      wg_ref, bg_ref, wout_ref, lng_ref, lnb_ref, y_ref):
    x = x_ref[0]
    xb = x.astype(BF16)
    hs = [[], [], []]
    for t in range(OUT_TILES):
        oa = oa_ref[0, :, t].reshape(WIDTH, TQ)
        ob = ob_ref[0, :, t].reshape(WIDTH, TQ)
        hs[0].append((oa * _silu(ga_ref[0, t])).astype(BF16))
        hs[1].append((ob * _silu(gb_ref[0, t])).astype(BF16))
        hs[2].append(om_ref[0, t])
    merged = None
    for j, w_ref in enumerate((wa_ref, wb_ref, wm_ref)):
        cols = slice(j * D_MODEL, (j + 1) * D_MODEL)
        z = jnp.dot(xb, wg_ref[:, cols], preferred_element_type=F32) + bg_ref[:, cols]
        y = jnp.concatenate([lax.dot_general(h, w_ref[...], _TN, preferred_element_type=F32)
                             for h in hs[j]], axis=0)
        term = _sigmoid(z) * y
        merged = term if merged is None else merged + term
    out = jnp.dot(merged.astype(BF16), wout_ref[...], preferred_element_type=F32)
    r = DEEPNORM_ALPHA * x + out
    mu = jnp.mean(r, axis=1, keepdims=True)
    rc = r - mu
    var = jnp.mean(rc * rc, axis=1, keepdims=True)
    y_ref[0] = rc * lax.rsqrt(var + LN_EPS) * lng_ref[...] + lnb_ref[...]


def _const_spec(shape):
    return pl.BlockSpec(shape, lambda *_: (0,) * len(shape))


def _params(n_axes):
    return pltpu.CompilerParams(dimension_semantics=("parallel",) * n_axes, vmem_limit_bytes=VMEM_LIMIT)


def _layer(x, mem, w_in, w_mem_kv, q_a_gain, w_q_b, kv_a_gain, w_kv_b, w_branch_mla, w_branch_sb,
           w_branch_mem, w_merge_gate, b_merge_gate, w_out, ln_gain, ln_bias):
    B, S, D = x.shape
    assert D == D_MODEL and S % (2 * TQ) == 0 and mem.shape == (B, MEM_LEN, D)
    NT = S // TQ
    half = MLA_ROPE // 2

    sb_scale = LOG2E / math.sqrt(SB_HEAD_DIM)
    wT = jnp.concatenate([w_in[:, O_QM:O_GM], w_in[:, O_CQ:O_KR], w_in[:, O_GA:O_QB],
                          w_in[:, O_QB:O_KB] * sb_scale, w_in[:, O_VB:O_QM], w_in[:, O_GM:O_END]],
                         axis=1).astype(BF16).T
    zeros96 = jnp.zeros((D, HEAD_PAD - MLA_ROPE), F32)
    w_rope = w_in[:, O_KR:O_GA]
    w_rot = jnp.concatenate([-w_rope[:, half:], w_rope[:, :half]], axis=1)
    wS = jnp.concatenate([w_in[:, O_CKV:O_KR], w_rope, zeros96, w_rot, zeros96, w_in[:, O_KB:O_VB]],
                         axis=1).astype(BF16)

    qd = MLA_NOPE + MLA_ROPE
    wq3 = w_q_b.reshape(MLA_Q_LORA, MLA_HEADS, qd)
    wqT = jnp.concatenate([wq3[:, :, :MLA_NOPE].reshape(MLA_Q_LORA, -1),
                           wq3[:, :, MLA_NOPE:MLA_NOPE + half].reshape(MLA_Q_LORA, -1),
                           wq3[:, :, MLA_NOPE + half:].reshape(MLA_Q_LORA, -1)], axis=1).astype(BF16).T
    kvd = MLA_NOPE + MLA_V
    wvT = w_kv_b.reshape(MLA_KV_LORA, MLA_HEADS, kvd)[:, :, MLA_NOPE:].reshape(MLA_KV_LORA, -1).astype(BF16).T
    knope_mask = np.zeros((1, MLA_HEADS * kvd), np.float32)
    place = np.zeros((HEAD_PAD, MLA_HEADS * HEAD_PAD), np.float32)
    for h in range(MLA_HEADS):
        knope_mask[0, h * kvd:h * kvd + MLA_NOPE] = 1.0
        place[np.arange(MLA_ROPE), h * HEAD_PAD + MLA_NOPE + np.arange(MLA_ROPE)] = 1.0
    wk = jnp.concatenate([w_kv_b * knope_mask, jnp.asarray(place)], axis=0).astype(BF16)

    qgain = q_a_gain.reshape(MLA_Q_LORA, 1)
    kvgain_col = kv_a_gain.reshape(MLA_KV_LORA, 1)
    kvgain_row = kv_a_gain.reshape(1, MLA_KV_LORA)

    freqs = ROPE_BASE ** (-jnp.arange(half, dtype=F32) / half)
    ang = jnp.arange(S, dtype=jnp.int32).astype(F32)[:, None] * freqs[None, :]
    cos, sin = jnp.cos(ang), jnp.sin(ang)
    cosq = jnp.tile(cos.T, (MLA_HEADS, 1))
    sinq = jnp.tile(sin.T, (MLA_HEADS, 1))
    zeros_k = jnp.zeros((S, HEAD_PAD - MLA_ROPE), F32)
    cosk = jnp.concatenate([cos, cos, zeros_k], axis=1)
    sink = jnp.concatenate([sin, sin, zeros_k], axis=1)

    wkm = w_mem_kv[:, :WIDTH].astype(BF16)
    wvmT = w_mem_kv[:, WIDTH:].T.astype(BF16)
    km, vmT = pl.pallas_call(
        _mem_kv_kernel,
        grid=(B,),
        in_specs=[pl.BlockSpec((1, MEM_LEN, D), lambda b: (b, 0, 0)),
                  _const_spec((D, WIDTH)), _const_spec((WIDTH, D))],
        out_specs=[pl.BlockSpec((1, MEM_LEN, WIDTH), lambda b: (b, 0, 0)),
                   pl.BlockSpec((1, WIDTH, MEM_LEN), lambda b: (b, 0, 0))],
        out_shape=[jax.ShapeDtypeStruct((B, MEM_LEN, WIDTH), BF16),
                   jax.ShapeDtypeStruct((B, WIDTH, MEM_LEN), BF16)],
        compiler_params=_params(1),
        name="mem_kv",
    )(mem, wkm, wvmT)

    head_q = lambda: pl.BlockSpec((1, MLA_HEADS, 1, HEAD_PAD, TQ), lambda b, t: (b, 0, t, 0, 0))
    head_v = lambda rows: pl.BlockSpec((1, MLA_HEADS, 1, rows, TQ), lambda b, t: (b, 0, t, 0, 0))
    wide = lambda: pl.BlockSpec((1, 1, WIDTH, TQ), lambda b, t: (b, t, 0, 0))
    qa, ka, va, qb, kb, vb, ga, gb, om = pl.pallas_call(
        _in_proj_kernel,
        grid=(B, NT),
        in_specs=[pl.BlockSpec((1, TQ, D), lambda b, t: (b, t, 0)),
                  _const_spec((R_END, D)), _const_spec((D, C_END)),
                  _const_spec((MLA_Q_LORA, 1)), _const_spec((MLA_KV_LORA, 1)), _const_spec((1, MLA_KV_LORA)),
                  _const_spec((MLA_HEADS * qd, MLA_Q_LORA)), _const_spec((WIDTH, MLA_KV_LORA)),
                  _const_spec((2 * HEAD_PAD, MLA_HEADS * HEAD_PAD)),
                  pl.BlockSpec((MLA_HEADS * half, TQ), lambda b, t: (0, t)),
                  pl.BlockSpec((MLA_HEADS * half, TQ), lambda b, t: (0, t)),
                  pl.BlockSpec((TQ, HEAD_PAD), lambda b, t: (t, 0)),
                  pl.BlockSpec((TQ, HEAD_PAD), lambda b, t: (t, 0)),
                  pl.BlockSpec((1, MEM_LEN, WIDTH), lambda b, t: (b, 0, 0)),
                  pl.BlockSpec((1, WIDTH, MEM_LEN), lambda b, t: (b, 0, 0))],
        out_specs=[head_q(),
                   pl.BlockSpec((1, TQ, MLA_HEADS * HEAD_PAD), lambda b, t: (b, t, 0)),
                   head_v(MLA_V_ROWS),
                   head_q(),
                   pl.BlockSpec((1, TQ, WIDTH), lambda b, t: (b, t, 0)),
                   head_v(SB_HEAD_DIM),
                   wide(), wide(), wide()],
        out_shape=[jax.ShapeDtypeStruct((B, MLA_HEADS, NT, HEAD_PAD, TQ), BF16),
                   jax.ShapeDtypeStruct((B, S, MLA_HEADS * HEAD_PAD), BF16),
                   jax.ShapeDtypeStruct((B, MLA_HEADS, NT, MLA_V_ROWS, TQ), BF16),
                   jax.ShapeDtypeStruct((B, SB_HEADS, NT, HEAD_PAD, TQ), BF16),
                   jax.ShapeDtypeStruct((B, S, WIDTH), BF16),
                   jax.ShapeDtypeStruct((B, SB_HEADS, NT, SB_HEAD_DIM, TQ), BF16),
                   jax.ShapeDtypeStruct((B, NT, WIDTH, TQ), F32),
                   jax.ShapeDtypeStruct((B, NT, WIDTH, TQ), F32),
                   jax.ShapeDtypeStruct((B, NT, WIDTH, TQ), BF16)],
        compiler_params=_params(2),
        name="in_proj",
    )(x, wT, wS, qgain, kvgain_col, kvgain_row, wqT, wvT, wk, cosq, sinq, cosk, sink, km, vmT)

    G = HEADS_PER_STEP
    group = lambda rows: pl.BlockSpec((1, G, NT, rows, TQ), lambda b, h: (b, h, 0, 0, 0))
    q_spec, v_spec = group(HEAD_PAD), group(MLA_V)
    o_shape = jax.ShapeDtypeStruct((B, MLA_HEADS, NT, MLA_V, TQ), F32)
    full = [(qi, kj) for qi in range(NT) for kj in range(qi)]
    assert NT % 4 == 0 and len(full) % 4 == 0
    tab_np = np.zeros((4, len(full) + 1), np.int32)
    tab_np[0:2, :] = NT - 1
    tab_np[0:2, :NT] = np.arange(NT)
    tab_np[2:4, :] = np.array(full[-1])[:, None]
    tab_np[2:4, :len(full)] = np.array(full).T
    oa = pl.pallas_call(
        _mla_attn_kernel,
        grid=(B, MLA_HEADS // G),
        in_specs=[pl.BlockSpec(memory_space=pltpu.SMEM),
                  q_spec, pl.BlockSpec((1, S, G * HEAD_PAD), lambda b, h: (b, 0, h)), group(MLA_V_ROWS)],
        out_specs=v_spec,
        out_shape=o_shape,
        scratch_shapes=[pltpu.VMEM((G, TQ, TQ), F32), pltpu.VMEM((G, TQ, TQ), F32),
                        pltpu.VMEM((NT * G, 1, TQ), F32), pltpu.VMEM((NT * G, MLA_V_ROWS, TQ), F32)],
        compiler_params=_params(2),
        name="mla_attn",
    )(jnp.asarray(tab_np), qa, ka, va)

    tri_np = np.triu(np.ones((TQ, TQ), np.float32))
    tri = jnp.asarray(np.concatenate([tri_np, tri_np], axis=1), dtype=BF16)
    ob = pl.pallas_call(
        _sb_attn_kernel,
        grid=(B, SB_HEADS // G),
        in_specs=[q_spec, pl.BlockSpec((1, S, G * SB_HEAD_DIM), lambda b, h: (b, 0, h)), v_spec,
                  _const_spec((TQ, 2 * TQ))],
        out_specs=v_spec,
        out_shape=o_shape,
        compiler_params=_params(2),
        name="sb_attn",
    )(qb, kb, vb, tri)

    TO = OUT_TILES
    o_in = lambda: pl.BlockSpec((1, MLA_HEADS, TO, MLA_V, TQ), lambda b, t: (b, 0, t, 0, 0))
    wide_o = lambda: pl.BlockSpec((1, TO, WIDTH, TQ), lambda b, t: (b, t, 0, 0))
    y = pl.pallas_call(
        _out_block_kernel,
        grid=(B, NT // TO),
        in_specs=[pl.BlockSpec((1, TO * TQ, D), lambda b, t: (b, t, 0)),
                  o_in(), o_in(), wide_o(), wide_o(), wide_o(),
                  _const_spec((WIDTH, D)), _const_spec((WIDTH, D)), _const_spec((WIDTH, D)),
                  _const_spec((D, 3 * D)), _const_spec((1, 3 * D)), _const_spec((D, D)),
                  _const_spec((1, D)), _const_spec((1, D))],
        out_specs=pl.BlockSpec((1, TO * TQ, D), lambda b, t: (b, t, 0)),
        out_shape=jax.ShapeDtypeStruct((B, S, D), F32),
        compiler_params=_params(2),
        name="out_block",
    )(x, oa, ob, ga, gb, om,
      w_branch_mla.astype(BF16), w_branch_sb.astype(BF16), w_branch_mem.astype(BF16),
      w_merge_gate.astype(BF16), b_merge_gate.reshape(1, 3 * D), w_out.astype(BF16),
      ln_gain.reshape(1, D), ln_bias.reshape(1, D))
    return y


def kernel(x, mem, w_in, w_mem_kv, q_a_gain, w_q_b, kv_a_gain, w_kv_b, w_branch_mla, w_branch_sb,
           w_branch_mem, w_merge_gate, b_merge_gate, w_out, ln_gain, ln_bias):
    h = x
    for l in range(w_in.shape[0]):
        h = _layer(h, mem, w_in[l], w_mem_kv[l], q_a_gain[l], w_q_b[l], kv_a_gain[l], w_kv_b[l],
                   w_branch_mla[l], w_branch_sb[l], w_branch_mem[l], w_merge_gate[l], b_merge_gate[l],
                   w_out[l], ln_gain[l], ln_bias[l])
    return h
```

```python
import functools
import math

import numpy as np
import jax
import jax.numpy as jnp
from jax import lax
from jax.experimental import pallas as pl
from jax.experimental.pallas import tpu as pltpu

F32 = jnp.float32
BF16 = jnp.bfloat16

D_MODEL = 1024
MEM_LEN = 256
MLA_HEADS, MLA_NOPE, MLA_ROPE, MLA_V = 8, 64, 32, 64
MLA_Q_LORA, MLA_KV_LORA = 256, 128
SB_HEADS, SB_HEAD_DIM = 8, 64
MEM_HEADS, MEM_HEAD_DIM = 4, 128
WIDTH = 512
ROPE_BASE = 10000.0
RMS_EPS = 1e-6
LN_EPS = 1e-5
DEPTH = 1
DEEPNORM_ALPHA = (2.0 * DEPTH) ** 0.25

_OFF = np.cumsum([0, MLA_Q_LORA, MLA_KV_LORA, MLA_ROPE, WIDTH, WIDTH, WIDTH, WIDTH, WIDTH, WIDTH, WIDTH])
(O_CQ, O_CKV, O_KR, O_GA, O_QB, O_KB, O_VB, O_GB, O_QM, O_GM, O_END) = [int(v) for v in _OFF]

TQ = 256
HEAD_PAD = 128
VMEM_LIMIT = 56 * 1024 * 1024
NEG = -1e30
HEADS_PER_STEP = 4
OUT_TILES = 1
MLA_DIAG_PAIRS_PER_TRIP = 8
MLA_FULL_PAIRS_PER_TRIP = 12
MLA_V_ROWS = 80
LOG2E = math.log2(math.e)
SB_DEAD_LOG2 = 152.0

R_QM, R_CQ, R_CKV, R_GA, R_QB, R_VB, R_GB, R_GM, R_END = [
    int(v) for v in np.cumsum([0, WIDTH, MLA_Q_LORA, MLA_KV_LORA, WIDTH, WIDTH, WIDTH, WIDTH, WIDTH])]
C_CKV, C_KB, C_ROPE, C_END = 0, 128, 640, 768

_NT = (((1,), (1,)), ((), ()))
_TN = (((0,), (0,)), ((), ()))


def _sigmoid(t):
    return 1.0 / (1.0 + jnp.exp(-t))


def _silu(t):
    return t * _sigmoid(t)


def _mem_kv_kernel(mem_ref, wk_ref, wvT_ref, km_ref, vmT_ref):
    mb = mem_ref[0].astype(BF16)
    km_ref[0] = jnp.dot(mb, wk_ref[...], preferred_element_type=F32).astype(BF16)
    vmT_ref[0] = lax.dot_general(wvT_ref[...], mb, _NT, preferred_element_type=F32).astype(BF16)


def _in_proj_kernel(x_ref, wT_ref, wS_ref, qgain_ref, kvgain_col_ref, kvgain_row_ref, wqT_ref, wvT_ref,
                    wk_ref, cosq_ref, sinq_ref, cosk_ref, sink_ref, km_ref, vmT_ref,
                    qa_ref, ka_ref, va_ref, qb_ref, kb_ref, vb_ref, ga_ref, gb_ref, om_ref):
    xb = x_ref[0].astype(BF16)

    def proj_T(lo, hi):
        return lax.dot_general(wT_ref[lo:hi], xb, _NT, preferred_element_type=F32)

    half_m = WIDTH // 2
    qmT = [proj_T(R_QM, R_QM + half_m), proj_T(R_QM + half_m, R_CQ)]
    pA = proj_T(R_CQ, R_GA)
    gaT = proj_T(R_GA, R_QB)
    km = km_ref[0]
    vmT = vmT_ref[0]
    inv_sqrt_d = 1.0 / math.sqrt(MEM_HEAD_DIM)
    heads_m = [slice(h * MEM_HEAD_DIM, (h + 1) * MEM_HEAD_DIM) for h in range(MEM_HEADS)]
    sm = []
    for h, sl in enumerate(heads_m):
        q_h = qmT[h // 2][(h % 2) * MEM_HEAD_DIM:(h % 2 + 1) * MEM_HEAD_DIM].astype(BF16)
        sm.append(jnp.dot(km[:, sl], q_h, preferred_element_type=F32))
    pS = jnp.dot(xb, wS_ref[...], preferred_element_type=F32)
    qbT, vbT, gbT, gmT = (proj_T(lo, lo + WIDTH) for lo in (R_QB, R_VB, R_GB, R_GM))

    for h, sl in enumerate(heads_m):
        s = sm[h] * inv_sqrt_d
        e = jnp.exp(s - jnp.max(s, axis=0, keepdims=True))
        inv_l = 1.0 / jnp.sum(e, axis=0, keepdims=True)
        o = jnp.dot(vmT[sl], e.astype(BF16), preferred_element_type=F32) * inv_l
        om_ref[0, 0, sl, :] = (o * _silu(gmT[sl])).astype(BF16)

    cq = pA[0:MLA_Q_LORA]
    nq = cq * lax.rsqrt(jnp.mean(cq * cq, axis=0, keepdims=True) + RMS_EPS) * qgain_ref[...]
    qaT = jnp.dot(wqT_ref[...], nq.astype(BF16), preferred_element_type=F32)
    scale = LOG2E / math.sqrt(MLA_NOPE + MLA_ROPE)
    n_nope = MLA_HEADS * MLA_NOPE
    half = MLA_ROPE // 2
    x1 = qaT[n_nope:n_nope + MLA_HEADS * half]
    x2 = qaT[n_nope + MLA_HEADS * half:]
    cq_t, sq_t = cosq_ref[...], sinq_ref[...]
    r1 = (x1 * cq_t - x2 * sq_t) * scale
    r2 = (x1 * sq_t + x2 * cq_t) * scale
    nope = qaT[:n_nope] * scale
    zpad = jnp.zeros((HEAD_PAD - MLA_NOPE - MLA_ROPE, TQ), BF16)
    for h in range(MLA_HEADS):
        qa_ref[0, h, 0, 0:MLA_NOPE, :] = nope[h * MLA_NOPE:(h + 1) * MLA_NOPE].astype(BF16)
        qa_ref[0, h, 0, MLA_NOPE:MLA_NOPE + half, :] = r1[h * half:(h + 1) * half].astype(BF16)
        qa_ref[0, h, 0, MLA_NOPE + half:MLA_NOPE + MLA_ROPE, :] = r2[h * half:(h + 1) * half].astype(BF16)
        qa_ref[0, h, 0, MLA_NOPE + MLA_ROPE:, :] = zpad

    ckvT = pA[MLA_Q_LORA:]
    nkvT = ckvT * lax.rsqrt(jnp.mean(ckvT * ckvT, axis=0, keepdims=True) + RMS_EPS) * kvgain_col_ref[...]
    vaT = jnp.dot(wvT_ref[...], nkvT.astype(BF16), preferred_element_type=F32)
    row = lax.broadcasted_iota(jnp.int32, (MLA_V_ROWS - MLA_V, TQ), 0)
    ones_row = jnp.where(row == 0, 1.0, 0.0).astype(BF16)
    for h in range(MLA_HEADS):
        va_ref[0, h, 0, 0:MLA_V, :] = vaT[h * MLA_V:(h + 1) * MLA_V].astype(BF16)
        va_ref[0, h, 0, MLA_V:, :] = ones_row

    ckv = pS[:, C_CKV:C_KB]
    nkv = ckv * lax.rsqrt(jnp.mean(ckv * ckv, axis=1, keepdims=True) + RMS_EPS) * kvgain_row_ref[...]
    kr = pS[:, C_ROPE:C_END]
    kpe = kr * cosk_ref[...] + pltpu.roll(kr, HEAD_PAD - MLA_ROPE, 1) * sink_ref[...]
    kin = jnp.concatenate([nkv.astype(BF16), kpe.astype(BF16)], axis=1)
    ka_ref[0] = jnp.dot(kin, wk_ref[...], preferred_element_type=F32).astype(BF16)

    kb_ref[0] = pS[:, C_KB:C_ROPE].astype(BF16)
    zhalf = jnp.zeros((HEAD_PAD - SB_HEAD_DIM, TQ), BF16)
    for h in range(SB_HEADS):
        lo = (h % 2) * SB_HEAD_DIM
        other = SB_HEAD_DIM - lo
        qb_ref[0, h, 0, lo:lo + SB_HEAD_DIM, :] = qbT[h * SB_HEAD_DIM:(h + 1) * SB_HEAD_DIM].astype(BF16)
        qb_ref[0, h, 0, other:other + SB_HEAD_DIM, :] = zhalf
        vb_ref[0, h, 0] = vbT[h * SB_HEAD_DIM:(h + 1) * SB_HEAD_DIM].astype(BF16)

    ga_ref[0, 0] = gaT
    gb_ref[0, 0] = gbT


def _mla_attn_kernel(tab_ref, q_ref, k_ref, v_ref, o_ref, sa_ref, sb_ref, m_ref, acc_ref):
    n_heads, n_tiles = q_ref.shape[1], q_ref.shape[2]
    key_idx = lax.broadcasted_iota(jnp.int32, (TQ, TQ), 0)
    qry_idx = lax.broadcasted_iota(jnp.int32, (TQ, TQ), 1)
    causal = key_idx <= qry_idx

    def sweep(row, n_visits, diagonal, pairs_per_trip):
        def fetch(dst_ref, i):
            qi, kj = tab_ref[row, i], tab_ref[row + 1, i]
            start = pl.multiple_of(kj * TQ, TQ)
            for g in range(n_heads):
                s = jnp.dot(k_ref[0, pl.ds(start, TQ), g * HEAD_PAD:(g + 1) * HEAD_PAD], q_ref[0, g, qi],
                            preferred_element_type=F32)
                dst_ref[g] = jnp.where(causal, s, NEG) if diagonal else s

        def consume(src_ref, i):
            qi, kj = tab_ref[row, i], tab_ref[row + 1, i]
            for g in range(n_heads):
                slot = qi * n_heads + g
                s = src_ref[g]
                m_new = jnp.max(s, axis=0, keepdims=True)
                if not diagonal:
                    m_old = m_ref[slot]
                    m_new = jnp.maximum(m_old, m_new)
                pv = jnp.dot(v_ref[0, g, kj], jnp.exp2(s - m_new).astype(BF16),
                             preferred_element_type=F32)
                acc_ref[slot] = pv if diagonal else jnp.exp2(m_old - m_new) * acc_ref[slot] + pv
                m_ref[slot] = m_new

        def pair(p):
            fetch(sb_ref, 2 * p + 1)
            consume(sa_ref, 2 * p)
            fetch(sa_ref, 2 * p + 2)
            consume(sb_ref, 2 * p + 1)

        def trip(t, carry):
            for j in range(pairs_per_trip):
                pair(pairs_per_trip * t + j)
            return carry

        assert n_visits % (2 * pairs_per_trip) == 0
        fetch(sa_ref, 0)
        lax.fori_loop(0, n_visits // (2 * pairs_per_trip), trip, 0)

    sweep(0, n_tiles, True, MLA_DIAG_PAIRS_PER_TRIP)
    sweep(2, n_tiles * (n_tiles - 1) // 2, False, MLA_FULL_PAIRS_PER_TRIP)

    def finish(qi, carry):
        for g in range(n_heads):
            acc = acc_ref[qi * n_heads + g]
            o_ref[0, g, qi] = acc[0:MLA_V] * (1.0 / acc[MLA_V:MLA_V + 1])
        return carry

    lax.fori_loop(0, n_tiles, finish, 0)


def _sb_attn_kernel(q_ref, k_ref, v_ref, tri_ref, o_ref):
    n_heads, n_tiles = q_ref.shape[1], q_ref.shape[2]
    key_idx = lax.broadcasted_iota(jnp.int32, (TQ, TQ), 0)
    qry_idx = lax.broadcasted_iota(jnp.int32, (TQ, TQ), 1)
    strict = key_idx < qry_idx

    def visit(chains, q_tiles, kjs, runs, masked):
        starts = [pl.multiple_of(kj * TQ, TQ) for kj in kjs]
        us, suffixes, new = [], [], []
        for g, slot, qT in chains:
            blk = (g // 2) * HEAD_PAD
            us.append(jnp.dot(k_ref[0, pl.ds(starts[slot], TQ), blk:blk + HEAD_PAD], qT,
                              preferred_element_type=F32))
        for u in us:
            sp2 = jnp.maximum(u, 0.0) + jnp.log(1.0 + jnp.exp2(-jnp.abs(u))) * LOG2E
            if masked:
                sp2 = jnp.where(strict, sp2, 0.0)
            hi = sp2.astype(BF16)
            lo = (sp2 - hi.astype(F32)).astype(BF16)
            suffixes.append(jnp.dot(tri_ref[...], jnp.concatenate([hi, lo], axis=0),
                                    preferred_element_type=F32))
        for c, (g, slot, _) in enumerate(chains):
            arg = (us[c] - runs[c]) - suffixes[c]
            if masked:
                arg = jnp.where(strict, arg, NEG)
            pv = jnp.dot(v_ref[0, g, kjs[slot]], jnp.exp2(arg).astype(BF16), preferred_element_type=F32)
            o_ref[0, g, q_tiles[slot]] = pv if masked else o_ref[0, g, q_tiles[slot]] + pv
            new.append(runs[c] + suffixes[c][0:1, :])
        return tuple(new)

    def all_dead(runs):
        run_min = runs[0]
        for run in runs[1:]:
            run_min = jnp.minimum(run_min, run)
        return jnp.min(run_min) > SB_DEAD_LOG2

    def pair_body(j, carry):
        q_tiles = (2 * j, 2 * j + 1)
        chains = [(g, slot, q_ref[0, g, q_tiles[slot]]) for slot in range(2) for g in range(n_heads)]
        runs = visit(chains, q_tiles, q_tiles, tuple(jnp.zeros((1, TQ), F32) for _ in chains), True)

        def cond(c):
            i, dead, _ = c
            return jnp.logical_and(i <= q_tiles[1], jnp.logical_not(dead))

        def body(c):
            i, _, runs = c
            spent = i > q_tiles[0]
            runs = tuple(run if slot else jnp.where(spent, -NEG, run)
                         for (_, slot, _), run in zip(chains, runs))
            new = visit(chains, q_tiles, (jnp.maximum(q_tiles[0] - i, 0), q_tiles[1] - i), runs, False)
            return i + 1, all_dead(new), new

        lax.while_loop(cond, body, (jnp.int32(1), all_dead(runs), runs))
        return carry

    lax.fori_loop(0, n_tiles // 2, pair_body, 0)


def _out_block_kernel(x_ref, oa_ref, ob_ref, ga_ref, gb_ref, om_ref, wa_ref, wb_ref, wm_ref,
                      wg_ref, bg_ref, wout_ref, lng_ref, lnb_ref, y_ref):
    x = x_ref[0]
    xb = x.astype(BF16)
    hs = [[], [], []]
    for t in range(OUT_TILES):
        oa = oa_ref[0, :, t].reshape(WIDTH, TQ)
        ob = ob_ref[0, :, t].reshape(WIDTH, TQ)
        hs[0].append((oa * _silu(ga_ref[0, t])).astype(BF16))
        hs[1].append((ob * _silu(gb_ref[0, t])).astype(BF16))
        hs[2].append(om_ref[0, t])
    merged = None
    for j, w_ref in enumerate((wa_ref, wb_ref, wm_ref)):
        cols = slice(j * D_MODEL, (j + 1) * D_MODEL)
        z = jnp.dot(xb, wg_ref[:, cols], preferred_element_type=F32) + bg_ref[:, cols]
        y = jnp.concatenate([lax.dot_general(h, w_ref[...], _TN, preferred_element_type=F32)
                             for h in hs[j]], axis=0)
        term = _sigmoid(z) * y
        merged = term if merged is None else merged + term
    out = jnp.dot(merged.astype(BF16), wout_ref[...], preferred_element_type=F32)
    r = DEEPNORM_ALPHA * x + out
    mu = jnp.mean(r, axis=1, keepdims=True)
    rc = r - mu
    var = jnp.mean(rc * rc, axis=1, keepdims=True)
    y_ref[0] = rc * lax.rsqrt(var + LN_EPS) * lng_ref[...] + lnb_ref[...]


def _const_spec(shape):
    return pl.BlockSpec(shape, lambda *_: (0,) * len(shape))


def _params(n_axes):
    return pltpu.CompilerParams(dimension_semantics=("parallel",) * n_axes, vmem_limit_bytes=VMEM_LIMIT)


def _layer(x, mem, w_in, w_mem_kv, q_a_gain, w_q_b, kv_a_gain, w_kv_b, w_branch_mla, w_branch_sb,
           w_branch_mem, w_merge_gate, b_merge_gate, w_out, ln_gain, ln_bias):
    B, S, D = x.shape
    assert D == D_MODEL and S % (2 * TQ) == 0 and mem.shape == (B, MEM_LEN, D)
    NT = S // TQ
    half = MLA_ROPE // 2

    sb_scale = LOG2E / math.sqrt(SB_HEAD_DIM)
    wT = jnp.concatenate([w_in[:, O_QM:O_GM], w_in[:, O_CQ:O_KR], w_in[:, O_GA:O_QB],
                          w_in[:, O_QB:O_KB] * sb_scale, w_in[:, O_VB:O_QM], w_in[:, O_GM:O_END]],
                         axis=1).astype(BF16).T
    w_rope = w_in[:, O_KR:O_GA]
    w_rot = jnp.concatenate([-w_rope[:, half:], w_rope[:, :half]], axis=1)
    wS = jnp.concatenate([w_in[:, O_CKV:O_KR], w_in[:, O_KB:O_VB], w_rope, w_rot,
                          jnp.zeros((D, HEAD_PAD - 2 * MLA_ROPE), F32)], axis=1).astype(BF16)

    qd = MLA_NOPE + MLA_ROPE
    wq3 = w_q_b.reshape(MLA_Q_LORA, MLA_HEADS, qd)
    wqT = jnp.concatenate([wq3[:, :, :MLA_NOPE].reshape(MLA_Q_LORA, -1),
                           wq3[:, :, MLA_NOPE:MLA_NOPE + half].reshape(MLA_Q_LORA, -1),
                           wq3[:, :, MLA_NOPE + half:].reshape(MLA_Q_LORA, -1)], axis=1).astype(BF16).T
    kvd = MLA_NOPE + MLA_V
    wvT = w_kv_b.reshape(MLA_KV_LORA, MLA_HEADS, kvd)[:, :, MLA_NOPE:].reshape(MLA_KV_LORA, -1).astype(BF16).T
    knope_mask = np.zeros((1, MLA_HEADS * kvd), np.float32)
    place = np.zeros((HEAD_PAD, MLA_HEADS * HEAD_PAD), np.float32)
    for h in range(MLA_HEADS):
        knope_mask[0, h * kvd:h * kvd + MLA_NOPE] = 1.0
        place[np.arange(MLA_ROPE), h * HEAD_PAD + MLA_NOPE + np.arange(MLA_ROPE)] = 1.0
    wk = jnp.concatenate([w_kv_b * knope_mask, jnp.asarray(place)], axis=0).astype(BF16)

    qgain = q_a_gain.reshape(MLA_Q_LORA, 1)
    kvgain_col = kv_a_gain.reshape(MLA_KV_LORA, 1)
    kvgain_row = kv_a_gain.reshape(1, MLA_KV_LORA)

    freqs = ROPE_BASE ** (-jnp.arange(half, dtype=F32) / half)
    ang = jnp.arange(S, dtype=jnp.int32).astype(F32)[:, None] * freqs[None, :]
    cos, sin = jnp.cos(ang), jnp.sin(ang)
    cosq = jnp.tile(cos.T, (MLA_HEADS, 1))
    sinq = jnp.tile(sin.T, (MLA_HEADS, 1))
    zeros_k = jnp.zeros((S, HEAD_PAD - MLA_ROPE), F32)
    cosk = jnp.concatenate([cos, cos, zeros_k], axis=1)
    sink = jnp.concatenate([sin, sin, zeros_k], axis=1)

    wkm = w_mem_kv[:, :WIDTH].astype(BF16)
    wvmT = w_mem_kv[:, WIDTH:].T.astype(BF16)
    km, vmT = pl.pallas_call(
        _mem_kv_kernel,
        grid=(B,),
        in_specs=[pl.BlockSpec((1, MEM_LEN, D), lambda b: (b, 0, 0)),
                  _const_spec((D, WIDTH)), _const_spec((WIDTH, D))],
        out_specs=[pl.BlockSpec((1, MEM_LEN, WIDTH), lambda b: (b, 0, 0)),
                   pl.BlockSpec((1, WIDTH, MEM_LEN), lambda b: (b, 0, 0))],
        out_shape=[jax.ShapeDtypeStruct((B, MEM_LEN, WIDTH), BF16),
                   jax.ShapeDtypeStruct((B, WIDTH, MEM_LEN), BF16)],
        compiler_params=_params(1),
        name="mem_kv",
    )(mem, wkm, wvmT)

    head_q = lambda: pl.BlockSpec((1, MLA_HEADS, 1, HEAD_PAD, TQ), lambda b, t: (b, 0, t, 0, 0))
    head_v = lambda rows: pl.BlockSpec((1, MLA_HEADS, 1, rows, TQ), lambda b, t: (b, 0, t, 0, 0))
    wide = lambda: pl.BlockSpec((1, 1, WIDTH, TQ), lambda b, t: (b, t, 0, 0))
    qa, ka, va, qb, kb, vb, ga, gb, om = pl.pallas_call(
        _in_proj_kernel,
        grid=(B, NT),
        in_specs=[pl.BlockSpec((1, TQ, D), lambda b, t: (b, t, 0)),
                  _const_spec((R_END, D)), _const_spec((D, C_END)),
                  _const_spec((MLA_Q_LORA, 1)), _const_spec((MLA_KV_LORA, 1)), _const_spec((1, MLA_KV_LORA)),
                  _const_spec((MLA_HEADS * qd, MLA_Q_LORA)), _const_spec((WIDTH, MLA_KV_LORA)),
                  _const_spec((2 * HEAD_PAD, MLA_HEADS * HEAD_PAD)),
                  pl.BlockSpec((MLA_HEADS * half, TQ), lambda b, t: (0, t)),
                  pl.BlockSpec((MLA_HEADS * half, TQ), lambda b, t: (0, t)),
                  pl.BlockSpec((TQ, HEAD_PAD), lambda b, t: (t, 0)),
                  pl.BlockSpec((TQ, HEAD_PAD), lambda b, t: (t, 0)),
                  pl.BlockSpec((1, MEM_LEN, WIDTH), lambda b, t: (b, 0, 0)),
                  pl.BlockSpec((1, WIDTH, MEM_LEN), lambda b, t: (b, 0, 0))],
        out_specs=[head_q(),
                   pl.BlockSpec((1, TQ, MLA_HEADS * HEAD_PAD), lambda b, t: (b, t, 0)),
                   head_v(MLA_V_ROWS),
                   head_q(),
                   pl.BlockSpec((1, TQ, WIDTH), lambda b, t: (b, t, 0)),
                   head_v(SB_HEAD_DIM),
                   wide(), wide(), wide()],
        out_shape=[jax.ShapeDtypeStruct((B, MLA_HEADS, NT, HEAD_PAD, TQ), BF16),
                   jax.ShapeDtypeStruct((B, S, MLA_HEADS * HEAD_PAD), BF16),
                   jax.ShapeDtypeStruct((B, MLA_HEADS, NT, MLA_V_ROWS, TQ), BF16),
                   jax.ShapeDtypeStruct((B, SB_HEADS, NT, HEAD_PAD, TQ), BF16),
                   jax.ShapeDtypeStruct((B, S, WIDTH), BF16),
                   jax.ShapeDtypeStruct((B, SB_HEADS, NT, SB_HEAD_DIM, TQ), BF16),
                   jax.ShapeDtypeStruct((B, NT, WIDTH, TQ), F32),
                   jax.ShapeDtypeStruct((B, NT, WIDTH, TQ), F32),
                   jax.ShapeDtypeStruct((B, NT, WIDTH, TQ), BF16)],
        compiler_params=_params(2),
        name="in_proj",
    )(x, wT, wS, qgain, kvgain_col, kvgain_row, wqT, wvT, wk, cosq, sinq, cosk, sink, km, vmT)

    G = HEADS_PER_STEP
    group = lambda rows: pl.BlockSpec((1, G, NT, rows, TQ), lambda b, h: (b, h, 0, 0, 0))
    q_spec, v_spec = group(HEAD_PAD), group(MLA_V)
    o_shape = jax.ShapeDtypeStruct((B, MLA_HEADS, NT, MLA_V, TQ), F32)
    full = [(qi, kj) for qi in range(NT) for kj in range(qi)]
    tab_np = np.zeros((4, len(full) + 1), np.int32)
    tab_np[0:2, :] = NT - 1
    tab_np[0:2, :NT] = np.arange(NT)
    tab_np[2:4, :] = np.array(full[-1])[:, None]
    tab_np[2:4, :len(full)] = np.array(full).T
    oa = pl.pallas_call(
        _mla_attn_kernel,
        grid=(B, MLA_HEADS // G),
        in_specs=[pl.BlockSpec(memory_space=pltpu.SMEM),
                  q_spec, pl.BlockSpec((1, S, G * HEAD_PAD), lambda b, h: (b, 0, h)), group(MLA_V_ROWS)],
        out_specs=v_spec,
        out_shape=o_shape,
        scratch_shapes=[pltpu.VMEM((G, TQ, TQ), F32), pltpu.VMEM((G, TQ, TQ), F32),
                        pltpu.VMEM((NT * G, 1, TQ), F32), pltpu.VMEM((NT * G, MLA_V_ROWS, TQ), F32)],
        compiler_params=_params(2),
        name="mla_attn",
    )(jnp.asarray(tab_np), qa, ka, va)

    tri_np = np.triu(np.ones((TQ, TQ), np.float32))
    tri = jnp.asarray(np.concatenate([tri_np, tri_np], axis=1), dtype=BF16)
    ob = pl.pallas_call(
        _sb_attn_kernel,
        grid=(B, SB_HEADS // G),
        in_specs=[q_spec, pl.BlockSpec((1, S, G * SB_HEAD_DIM), lambda b, h: (b, 0, h)), v_spec,
                  _const_spec((TQ, 2 * TQ))],
        out_specs=v_spec,
        out_shape=o_shape,
        compiler_params=_params(2),
        name="sb_attn",
    )(qb, kb, vb, tri)

    TO = OUT_TILES
    o_in = lambda: pl.BlockSpec((1, MLA_HEADS, TO, MLA_V, TQ), lambda b, t: (b, 0, t, 0, 0))
    wide_o = lambda: pl.BlockSpec((1, TO, WIDTH, TQ), lambda b, t: (b, t, 0, 0))
    y = pl.pallas_call(
        _out_block_kernel,
        grid=(B, NT // TO),
        in_specs=[pl.BlockSpec((1, TO * TQ, D), lambda b, t: (b, t, 0)),
                  o_in(), o_in(), wide_o(), wide_o(), wide_o(),
                  _const_spec((WIDTH, D)), _const_spec((WIDTH, D)), _const_spec((WIDTH, D)),
                  _const_spec((D, 3 * D)), _const_spec((1, 3 * D)), _const_spec((D, D)),
                  _const_spec((1, D)), _const_spec((1, D))],
        out_specs=pl.BlockSpec((1, TO * TQ, D), lambda b, t: (b, t, 0)),
        out_shape=jax.ShapeDtypeStruct((B, S, D), F32),
        compiler_params=_params(2),
        name="out_block",
    )(x, oa, ob, ga, gb, om,
      w_branch_mla.astype(BF16), w_branch_sb.astype(BF16), w_branch_mem.astype(BF16),
      w_merge_gate.astype(BF16), b_merge_gate.reshape(1, 3 * D), w_out.astype(BF16),
      ln_gain.reshape(1, D), ln_bias.reshape(1, D))
    return y


def kernel(x, mem, w_in, w_mem_kv, q_a_gain, w_q_b, kv_a_gain, w_kv_b, w_branch_mla, w_branch_sb,
           w_branch_mem, w_merge_gate, b_merge_gate, w_out, ln_gain, ln_bias):
    h = x
    for l in range(w_in.shape[0]):
        h = _layer(h, mem, w_in[l], w_mem_kv[l], q_a_gain[l], w_q_b[l], kv_a_gain[l], w_kv_b[l],
                   w_branch_mla[l], w_branch_sb[l], w_branch_mem[l], w_merge_gate[l], b_merge_gate[l],
                   w_out[l], ln_gain[l], ln_bias[l])
    return h
```

```python
import functools
import math

import numpy as np
import jax
import jax.numpy as jnp
from jax import lax
from jax.experimental import pallas as pl
from jax.experimental.pallas import tpu as pltpu

F32 = jnp.float32
BF16 = jnp.bfloat16

D_MODEL = 1024
MEM_LEN = 256
MLA_HEADS, MLA_NOPE, MLA_ROPE, MLA_V = 8, 64, 32, 64
MLA_Q_LORA, MLA_KV_LORA = 256, 128
SB_HEADS, SB_HEAD_DIM = 8, 64
MEM_HEADS, MEM_HEAD_DIM = 4, 128
WIDTH = 512
ROPE_BASE = 10000.0
RMS_EPS = 1e-6
LN_EPS = 1e-5
DEPTH = 1
DEEPNORM_ALPHA = (2.0 * DEPTH) ** 0.25

_OFF = np.cumsum([0, MLA_Q_LORA, MLA_KV_LORA, MLA_ROPE, WIDTH, WIDTH, WIDTH, WIDTH, WIDTH, WIDTH, WIDTH])
(O_CQ, O_CKV, O_KR, O_GA, O_QB, O_KB, O_VB, O_GB, O_QM, O_GM, O_END) = [int(v) for v in _OFF]

TQ = 256
HEAD_PAD = 128
VMEM_LIMIT = 56 * 1024 * 1024
NEG = -1e30
HEADS_PER_STEP = 4
MLA_DIAG_PAIRS_PER_TRIP = 8
MLA_FULL_PAIRS_PER_TRIP = 12
MLA_V_ROWS = 80
LOG2E = math.log2(math.e)
SB_DEAD_LOG2 = 152.0

R_QM, R_CQ, R_CKV, R_GA, R_QB, R_VB, R_GB, R_GM, R_END = [
    int(v) for v in np.cumsum([0, WIDTH, MLA_Q_LORA, MLA_KV_LORA, WIDTH, WIDTH, WIDTH, WIDTH, WIDTH])]
C_CKV, C_KB, C_ROPE, C_END = 0, 128, 640, 768

_NT = (((1,), (1,)), ((), ()))
_TN = (((0,), (0,)), ((), ()))


def _sigmoid(t):
    return 1.0 / (1.0 + jnp.exp(-t))


def _silu(t):
    return t * _sigmoid(t)


def _mem_kv_kernel(mem_ref, wk_ref, wvT_ref, km_ref, vmT_ref):
    mb = mem_ref[0].astype(BF16)
    km_ref[0] = jnp.dot(mb, wk_ref[...], preferred_element_type=F32).astype(BF16)
    vmT_ref[0] = lax.dot_general(wvT_ref[...], mb, _NT, preferred_element_type=F32).astype(BF16)


def _in_proj_kernel(x_ref, wT_ref, wS_ref, qgain_ref, kvgain_col_ref, kvgain_row_ref, wqT_ref, wvT_ref,
                    wk_ref, cosq_ref, sinq_ref, cosk_ref, sink_ref, km_ref, vmT_ref,
                    qa_ref, ka_ref, va_ref, qb_ref, kb_ref, vb_ref, ga_ref, gb_ref, om_ref):
    xb = x_ref[0].astype(BF16)

    def proj_T(lo, hi):
        return lax.dot_general(wT_ref[lo:hi], xb, _NT, preferred_element_type=F32)

    half_m = WIDTH // 2
    qmT = [proj_T(R_QM, R_QM + half_m), proj_T(R_QM + half_m, R_CQ)]
    pA = proj_T(R_CQ, R_GA)
    gaT = proj_T(R_GA, R_QB)
    km = km_ref[0]
    vmT = vmT_ref[0]
    inv_sqrt_d = 1.0 / math.sqrt(MEM_HEAD_DIM)
    heads_m = [slice(h * MEM_HEAD_DIM, (h + 1) * MEM_HEAD_DIM) for h in range(MEM_HEADS)]
    sm = []
    for h, sl in enumerate(heads_m):
        q_h = qmT[h // 2][(h % 2) * MEM_HEAD_DIM:(h % 2 + 1) * MEM_HEAD_DIM].astype(BF16)
        sm.append(jnp.dot(km[:, sl], q_h, preferred_element_type=F32))
    pS = jnp.dot(xb, wS_ref[...], preferred_element_type=F32)
    qbT, vbT, gbT, gmT = (proj_T(lo, lo + WIDTH) for lo in (R_QB, R_VB, R_GB, R_GM))

    for h, sl in enumerate(heads_m):
        s = sm[h] * inv_sqrt_d
        e = jnp.exp(s - jnp.max(s, axis=0, keepdims=True))
        inv_l = 1.0 / jnp.sum(e, axis=0, keepdims=True)
        o = jnp.dot(vmT[sl], e.astype(BF16), preferred_element_type=F32) * inv_l
        om_ref[0, 0, sl, :] = (o * _silu(gmT[sl])).astype(BF16)

    cq = pA[0:MLA_Q_LORA]
    nq = cq * lax.rsqrt(jnp.mean(cq * cq, axis=0, keepdims=True) + RMS_EPS) * qgain_ref[...]
    qaT = jnp.dot(wqT_ref[...], nq.astype(BF16), preferred_element_type=F32)
    scale = LOG2E / math.sqrt(MLA_NOPE + MLA_ROPE)
    n_nope = MLA_HEADS * MLA_NOPE
    half = MLA_ROPE // 2
    x1 = qaT[n_nope:n_nope + MLA_HEADS * half]
    x2 = qaT[n_nope + MLA_HEADS * half:]
    cq_t, sq_t = cosq_ref[...], sinq_ref[...]
    r1 = (x1 * cq_t - x2 * sq_t) * scale
    r2 = (x1 * sq_t + x2 * cq_t) * scale
    nope = qaT[:n_nope] * scale
    zpad = jnp.zeros((HEAD_PAD - MLA_NOPE - MLA_ROPE, TQ), BF16)
    for h in range(MLA_HEADS):
        qa_ref[0, h, 0, 0:MLA_NOPE, :] = nope[h * MLA_NOPE:(h + 1) * MLA_NOPE].astype(BF16)
        qa_ref[0, h, 0, MLA_NOPE:MLA_NOPE + half, :] = r1[h * half:(h + 1) * half].astype(BF16)
        qa_ref[0, h, 0, MLA_NOPE + half:MLA_NOPE + MLA_ROPE, :] = r2[h * half:(h + 1) * half].astype(BF16)
        qa_ref[0, h, 0, MLA_NOPE + MLA_ROPE:, :] = zpad

    ckvT = pA[MLA_Q_LORA:]
    nkvT = ckvT * lax.rsqrt(jnp.mean(ckvT * ckvT, axis=0, keepdims=True) + RMS_EPS) * kvgain_col_ref[...]
    vaT = jnp.dot(wvT_ref[...], nkvT.astype(BF16), preferred_element_type=F32)
    row = lax.broadcasted_iota(jnp.int32, (MLA_V_ROWS - MLA_V, TQ), 0)
    ones_row = jnp.where(row == 0, 1.0, 0.0).astype(BF16)
    for h in range(MLA_HEADS):
        va_ref[0, h, 0, 0:MLA_V, :] = vaT[h * MLA_V:(h + 1) * MLA_V].astype(BF16)
        va_ref[0, h, 0, MLA_V:, :] = ones_row

    ckv = pS[:, C_CKV:C_KB]
    nkv = ckv * lax.rsqrt(jnp.mean(ckv * ckv, axis=1, keepdims=True) + RMS_EPS) * kvgain_row_ref[...]
    kr = pS[:, C_ROPE:C_END]
    kpe = kr * cosk_ref[...] + pltpu.roll(kr, HEAD_PAD - MLA_ROPE, 1) * sink_ref[...]
    kin = jnp.concatenate([nkv.astype(BF16), kpe.astype(BF16)], axis=1)
    ka_ref[0] = jnp.dot(kin, wk_ref[...], preferred_element_type=F32).astype(BF16)

    kb_ref[0] = pS[:, C_KB:C_ROPE].astype(BF16)
    zhalf = jnp.zeros((HEAD_PAD - SB_HEAD_DIM, TQ), BF16)
    for h in range(SB_HEADS):
        lo = (h % 2) * SB_HEAD_DIM
        other = SB_HEAD_DIM - lo
        qb_ref[0, h, 0, lo:lo + SB_HEAD_DIM, :] = qbT[h * SB_HEAD_DIM:(h + 1) * SB_HEAD_DIM].astype(BF16)
        qb_ref[0, h, 0, other:other + SB_HEAD_DIM, :] = zhalf
        vb_ref[0, h, 0] = vbT[h * SB_HEAD_DIM:(h + 1) * SB_HEAD_DIM].astype(BF16)

    ga_ref[0, 0] = gaT
    gb_ref[0, 0] = gbT


def _mla_attn_kernel(tab_ref, q_ref, k_ref, v_ref, o_ref, sa_ref, sb_ref, m_ref, acc_ref):
    n_heads, n_tiles = q_ref.shape[1], q_ref.shape[2]
    key_idx = lax.broadcasted_iota(jnp.int32, (TQ, TQ), 0)
    qry_idx = lax.broadcasted_iota(jnp.int32, (TQ, TQ), 1)
    causal = key_idx <= qry_idx

    def sweep(row, n_visits, diagonal, pairs_per_trip):
        def fetch(dst_ref, i):
            qi, kj = tab_ref[row, i], tab_ref[row + 1, i]
            start = pl.multiple_of(kj * TQ, TQ)
            for g in range(n_heads):
                s = jnp.dot(k_ref[0, pl.ds(start, TQ), g * HEAD_PAD:(g + 1) * HEAD_PAD], q_ref[0, g, qi],
                            preferred_element_type=F32)
                dst_ref[g] = jnp.where(causal, s, NEG) if diagonal else s

        def consume(src_ref, i):
            qi, kj = tab_ref[row, i], tab_ref[row + 1, i]
            for g in range(n_heads):
                slot = qi * n_heads + g
                s = src_ref[g]
                m_new = jnp.max(s, axis=0, keepdims=True)
                if not diagonal:
                    m_old = m_ref[slot]
                    m_new = jnp.maximum(m_old, m_new)
                pv = jnp.dot(v_ref[0, g, kj], jnp.exp2(s - m_new).astype(BF16),
                             preferred_element_type=F32)
                acc_ref[slot] = pv if diagonal else jnp.exp2(m_old - m_new) * acc_ref[slot] + pv
                m_ref[slot] = m_new

        def pair(p):
            fetch(sb_ref, 2 * p + 1)
            consume(sa_ref, 2 * p)
            fetch(sa_ref, 2 * p + 2)
            consume(sb_ref, 2 * p + 1)

        def trip(t, carry):
            for j in range(pairs_per_trip):
                pair(pairs_per_trip * t + j)
            return carry

        assert n_visits % (2 * pairs_per_trip) == 0
        fetch(sa_ref, 0)
        lax.fori_loop(0, n_visits // (2 * pairs_per_trip), trip, 0)

    sweep(0, n_tiles, True, MLA_DIAG_PAIRS_PER_TRIP)
    sweep(2, n_tiles * (n_tiles - 1) // 2, False, MLA_FULL_PAIRS_PER_TRIP)

    def finish(qi, carry):
        for g in range(n_heads):
            acc = acc_ref[qi * n_heads + g]
            o_ref[0, g, qi] = acc[0:MLA_V] * (1.0 / acc[MLA_V:MLA_V + 1])
        return carry

    lax.fori_loop(0, n_tiles, finish, 0)


def _sb_attn_kernel(q_ref, k_ref, v_ref, tri_ref, o_ref):
    n_heads, n_tiles = q_ref.shape[1], q_ref.shape[2]
    key_idx = lax.broadcasted_iota(jnp.int32, (TQ, TQ), 0)
    qry_idx = lax.broadcasted_iota(jnp.int32, (TQ, TQ), 1)
    strict = key_idx < qry_idx

    def scores(chains, kjs):
        starts = [pl.multiple_of(kj * TQ, TQ) for kj in kjs]
        us = []
        for g, slot, qT in chains:
            blk = (g // 2) * HEAD_PAD
            us.append(jnp.dot(k_ref[0, pl.ds(starts[slot], TQ), blk:blk + HEAD_PAD], qT,
                              preferred_element_type=F32))
        return us

    def suffix_sums(us, masked):
        suffixes = []
        for u in us:
            sp2 = jnp.maximum(u, 0.0) + jnp.log(1.0 + jnp.exp2(-jnp.abs(u))) * LOG2E
            if masked:
                sp2 = jnp.where(strict, sp2, 0.0)
            hi = sp2.astype(BF16)
            lo = (sp2 - hi.astype(F32)).astype(BF16)
            suffixes.append(jnp.dot(tri_ref[...], jnp.concatenate([hi, lo], axis=0),
                                    preferred_element_type=F32))
        return suffixes

    def accumulate(chains, q_tiles, kjs, us, suffixes, runs, masked):
        new = []
        for c, (g, slot, _) in enumerate(chains):
            arg = us[c] - suffixes[c]
            if masked:
                arg = jnp.where(strict, arg, NEG)
            pv = jnp.dot(v_ref[0, g, kjs[slot]], jnp.exp2(arg).astype(BF16), preferred_element_type=F32)
            if not masked:
                pv = pv * jnp.exp2(-runs[c])
            o_ref[0, g, q_tiles[slot]] = pv if masked else o_ref[0, g, q_tiles[slot]] + pv
            new.append(runs[c] + suffixes[c][0:1, :])
        return tuple(new)

    def visit(chains, q_tiles, kjs, runs, masked):
        us = scores(chains, kjs)
        return accumulate(chains, q_tiles, kjs, us, suffix_sums(us, masked), runs, masked)

    def all_dead(runs):
        run_min = runs[0]
        for run in runs[1:]:
            run_min = jnp.minimum(run_min, run)
        return jnp.min(run_min) > SB_DEAD_LOG2

    def pair_body(j, carry):
        q_tiles = (2 * j, 2 * j + 1)
        chains = [(g, slot, q_ref[0, g, q_tiles[slot]]) for slot in range(2) for g in range(n_heads)]
        def key_tiles(i):
            return jnp.maximum(q_tiles[0] - i, 0), q_tiles[1] - i

        def spend(i, runs):
            spent = i > q_tiles[0]
            return tuple(run if slot else jnp.where(spent, -NEG, run)
                         for (_, slot, _), run in zip(chains, runs))

        us0, us1 = scores(chains, q_tiles), scores(chains, key_tiles(1))
        runs = accumulate(chains, q_tiles, q_tiles, us0, suffix_sums(us0, True),
                          tuple(jnp.zeros((1, TQ), F32) for _ in chains), True)
        runs = accumulate(chains, q_tiles, key_tiles(1), us1, suffix_sums(us1, False), spend(1, runs), False)

        def cond(c):
            i, dead, _ = c
            return jnp.logical_and(i <= q_tiles[1], jnp.logical_not(dead))

        def body(c):
            i, _, runs = c
            new = visit(chains, q_tiles, key_tiles(i), spend(i, runs), False)
            return i + 1, all_dead(new), new

        lax.while_loop(cond, body, (jnp.int32(2), all_dead(runs), runs))
        return carry

    lax.fori_loop(0, n_tiles // 2, pair_body, 0)


def _out_block_kernel(x_ref, oa_ref, ob_ref, ga_ref, gb_ref, om_ref, wa_ref, wb_ref, wm_ref,
                      wg_ref, bg_ref, wout_ref, lng_ref, lnb_ref, y_ref):
    x = x_ref[0]
    xb = x.astype(BF16)
    oa = oa_ref[0, :, 0].reshape(WIDTH, TQ)
    ob = ob_ref[0, :, 0].reshape(WIDTH, TQ)
    hs = [(oa * _silu(ga_ref[0, 0])).astype(BF16), (ob * _silu(gb_ref[0, 0])).astype(BF16), om_ref[0, 0]]
    merged = None
    for j, (w_ref, h) in enumerate(zip((wa_ref, wb_ref, wm_ref), hs)):
        cols = slice(j * D_MODEL, (j + 1) * D_MODEL)
        z = jnp.dot(xb, wg_ref[:, cols], preferred_element_type=F32) + bg_ref[:, cols]
        term = _sigmoid(z) * lax.dot_general(h, w_ref[...], _TN, preferred_element_type=F32)
        merged = term if merged is None else merged + term
    out = jnp.dot(merged.astype(BF16), wout_ref[...], preferred_element_type=F32)
    r = DEEPNORM_ALPHA * x + out
    mu = jnp.mean(r, axis=1, keepdims=True)
    rc = r - mu
    var = jnp.mean(rc * rc, axis=1, keepdims=True)
    y_ref[0] = rc * lax.rsqrt(var + LN_EPS) * lng_ref[...] + lnb_ref[...]


def _const_spec(shape):
    return pl.BlockSpec(shape, lambda *_: (0,) * len(shape))


def _params(n_axes):
    return pltpu.CompilerParams(dimension_semantics=("parallel",) * n_axes, vmem_limit_bytes=VMEM_LIMIT)


def _layer(x, mem, w_in, w_mem_kv, q_a_gain, w_q_b, kv_a_gain, w_kv_b, w_branch_mla, w_branch_sb,
           w_branch_mem, w_merge_gate, b_merge_gate, w_out, ln_gain, ln_bias):
    B, S, D = x.shape
    assert D == D_MODEL and S % (2 * TQ) == 0 and mem.shape == (B, MEM_LEN, D)
    NT = S // TQ
    half = MLA_ROPE // 2

    sb_scale = LOG2E / math.sqrt(SB_HEAD_DIM)
    wT = jnp.concatenate([w_in[:, O_QM:O_GM], w_in[:, O_CQ:O_KR], w_in[:, O_GA:O_QB],
                          w_in[:, O_QB:O_KB] * sb_scale, w_in[:, O_VB:O_QM], w_in[:, O_GM:O_END]],
                         axis=1).astype(BF16).T
    w_rope = w_in[:, O_KR:O_GA]
    w_rot = jnp.concatenate([-w_rope[:, half:], w_rope[:, :half]], axis=1)
    wS = jnp.concatenate([w_in[:, O_CKV:O_KR], w_in[:, O_KB:O_VB], w_rope, w_rot,
                          jnp.zeros((D, HEAD_PAD - 2 * MLA_ROPE), F32)], axis=1).astype(BF16)

    qd = MLA_NOPE + MLA_ROPE
    wq3 = w_q_b.reshape(MLA_Q_LORA, MLA_HEADS, qd)
    wqT = jnp.concatenate([wq3[:, :, :MLA_NOPE].reshape(MLA_Q_LORA, -1),
                           wq3[:, :, MLA_NOPE:MLA_NOPE + half].reshape(MLA_Q_LORA, -1),
                           wq3[:, :, MLA_NOPE + half:].reshape(MLA_Q_LORA, -1)], axis=1).astype(BF16).T
    kvd = MLA_NOPE + MLA_V
    wvT = w_kv_b.reshape(MLA_KV_LORA, MLA_HEADS, kvd)[:, :, MLA_NOPE:].reshape(MLA_KV_LORA, -1).astype(BF16).T
    knope_mask = np.zeros((1, MLA_HEADS * kvd), np.float32)
    place = np.zeros((HEAD_PAD, MLA_HEADS * HEAD_PAD), np.float32)
    for h in range(MLA_HEADS):
        knope_mask[0, h * kvd:h * kvd + MLA_NOPE] = 1.0
        place[np.arange(MLA_ROPE), h * HEAD_PAD + MLA_NOPE + np.arange(MLA_ROPE)] = 1.0
    wk = jnp.concatenate([w_kv_b * knope_mask, jnp.asarray(place)], axis=0).astype(BF16)

    qgain = q_a_gain.reshape(MLA_Q_LORA, 1)
    kvgain_col = kv_a_gain.reshape(MLA_KV_LORA, 1)
    kvgain_row = kv_a_gain.reshape(1, MLA_KV_LORA)

    freqs = ROPE_BASE ** (-jnp.arange(half, dtype=F32) / half)
    ang = jnp.arange(S, dtype=jnp.int32).astype(F32)[:, None] * freqs[None, :]
    cos, sin = jnp.cos(ang), jnp.sin(ang)
    cosq = jnp.tile(cos.T, (MLA_HEADS, 1))
    sinq = jnp.tile(sin.T, (MLA_HEADS, 1))
    zeros_k = jnp.zeros((S, HEAD_PAD - MLA_ROPE), F32)
    cosk = jnp.concatenate([cos, cos, zeros_k], axis=1)
    sink = jnp.concatenate([sin, sin, zeros_k], axis=1)

    wkm = w_mem_kv[:, :WIDTH].astype(BF16)
    wvmT = w_mem_kv[:, WIDTH:].T.astype(BF16)
    km, vmT = pl.pallas_call(
        _mem_kv_kernel,
        grid=(B,),
        in_specs=[pl.BlockSpec((1, MEM_LEN, D), lambda b: (b, 0, 0)),
                  _const_spec((D, WIDTH)), _const_spec((WIDTH, D))],
        out_specs=[pl.BlockSpec((1, MEM_LEN, WIDTH), lambda b: (b, 0, 0)),
                   pl.BlockSpec((1, WIDTH, MEM_LEN), lambda b: (b, 0, 0))],
        out_shape=[jax.ShapeDtypeStruct((B, MEM_LEN, WIDTH), BF16),
                   jax.ShapeDtypeStruct((B, WIDTH, MEM_LEN), BF16)],
        compiler_params=_params(1),
        name="mem_kv",
    )(mem, wkm, wvmT)

    head_q = lambda: pl.BlockSpec((1, MLA_HEADS, 1, HEAD_PAD, TQ), lambda b, t: (b, 0, t, 0, 0))
    head_v = lambda rows: pl.BlockSpec((1, MLA_HEADS, 1, rows, TQ), lambda b, t: (b, 0, t, 0, 0))
    wide = lambda: pl.BlockSpec((1, 1, WIDTH, TQ), lambda b, t: (b, t, 0, 0))
    qa, ka, va, qb, kb, vb, ga, gb, om = pl.pallas_call(
        _in_proj_kernel,
        grid=(B, NT),
        in_specs=[pl.BlockSpec((1, TQ, D), lambda b, t: (b, t, 0)),
                  _const_spec((R_END, D)), _const_spec((D, C_END)),
                  _const_spec((MLA_Q_LORA, 1)), _const_spec((MLA_KV_LORA, 1)), _const_spec((1, MLA_KV_LORA)),
                  _const_spec((MLA_HEADS * qd, MLA_Q_LORA)), _const_spec((WIDTH, MLA_KV_LORA)),
                  _const_spec((2 * HEAD_PAD, MLA_HEADS * HEAD_PAD)),
                  pl.BlockSpec((MLA_HEADS * half, TQ), lambda b, t: (0, t)),
                  pl.BlockSpec((MLA_HEADS * half, TQ), lambda b, t: (0, t)),
                  pl.BlockSpec((TQ, HEAD_PAD), lambda b, t: (t, 0)),
                  pl.BlockSpec((TQ, HEAD_PAD), lambda b, t: (t, 0)),
                  pl.BlockSpec((1, MEM_LEN, WIDTH), lambda b, t: (b, 0, 0)),
                  pl.BlockSpec((1, WIDTH, MEM_LEN), lambda b, t: (b, 0, 0))],
        out_specs=[head_q(),
                   pl.BlockSpec((1, TQ, MLA_HEADS * HEAD_PAD), lambda b, t: (b, t, 0)),
                   head_v(MLA_V_ROWS),
                   head_q(),
                   pl.BlockSpec((1, TQ, WIDTH), lambda b, t: (b, t, 0)),
                   head_v(SB_HEAD_DIM),
                   wide(), wide(), wide()],
        out_shape=[jax.ShapeDtypeStruct((B, MLA_HEADS, NT, HEAD_PAD, TQ), BF16),
                   jax.ShapeDtypeStruct((B, S, MLA_HEADS * HEAD_PAD), BF16),
                   jax.ShapeDtypeStruct((B, MLA_HEADS, NT, MLA_V_ROWS, TQ), BF16),
                   jax.ShapeDtypeStruct((B, SB_HEADS, NT, HEAD_PAD, TQ), BF16),
                   jax.ShapeDtypeStruct((B, S, WIDTH), BF16),
                   jax.ShapeDtypeStruct((B, SB_HEADS, NT, SB_HEAD_DIM, TQ), BF16),
                   jax.ShapeDtypeStruct((B, NT, WIDTH, TQ), F32),
                   jax.ShapeDtypeStruct((B, NT, WIDTH, TQ), F32),
                   jax.ShapeDtypeStruct((B, NT, WIDTH, TQ), BF16)],
        compiler_params=_params(2),
        name="in_proj",
    )(x, wT, wS, qgain, kvgain_col, kvgain_row, wqT, wvT, wk, cosq, sinq, cosk, sink, km, vmT)

    G = HEADS_PER_STEP
    group = lambda rows: pl.BlockSpec((1, G, NT, rows, TQ), lambda b, h: (b, h, 0, 0, 0))
    q_spec, v_spec = group(HEAD_PAD), group(MLA_V)
    o_shape = jax.ShapeDtypeStruct((B, MLA_HEADS, NT, MLA_V, TQ), F32)
    full = [(qi, kj) for qi in range(NT) for kj in range(qi)]
    tab_np = np.zeros((4, len(full) + 1), np.int32)
    tab_np[0:2, :] = NT - 1
    tab_np[0:2, :NT] = np.arange(NT)
    tab_np[2:4, :] = np.array(full[-1])[:, None]
    tab_np[2:4, :len(full)] = np.array(full).T
    oa = pl.pallas_call(
        _mla_attn_kernel,
        grid=(B, MLA_HEADS // G),
        in_specs=[pl.BlockSpec(memory_space=pltpu.SMEM),
                  q_spec, pl.BlockSpec((1, S, G * HEAD_PAD), lambda b, h: (b, 0, h)), group(MLA_V_ROWS)],
        out_specs=v_spec,
        out_shape=o_shape,
        scratch_shapes=[pltpu.VMEM((G, TQ, TQ), F32), pltpu.VMEM((G, TQ, TQ), F32),
                        pltpu.VMEM((NT * G, 1, TQ), F32), pltpu.VMEM((NT * G, MLA_V_ROWS, TQ), F32)],
        compiler_params=_params(2),
        name="mla_attn",
    )(jnp.asarray(tab_np), qa, ka, va)

    tri_np = np.triu(np.ones((TQ, TQ), np.float32))
    tri = jnp.asarray(np.concatenate([tri_np, tri_np], axis=1), dtype=BF16)
    ob = pl.pallas_call(
        _sb_attn_kernel,
        grid=(B, SB_HEADS // G),
        in_specs=[q_spec, pl.BlockSpec((1, S, G * SB_HEAD_DIM), lambda b, h: (b, 0, h)), v_spec,
                  _const_spec((TQ, 2 * TQ))],
        out_specs=v_spec,
        out_shape=o_shape,
        compiler_params=_params(2),
        name="sb_attn",
    )(qb, kb, vb, tri)

    o_in = lambda: pl.BlockSpec((1, MLA_HEADS, 1, MLA_V, TQ), lambda b, t: (b, 0, t, 0, 0))
    y = pl.pallas_call(
        _out_block_kernel,
        grid=(B, NT),
        in_specs=[pl.BlockSpec((1, TQ, D), lambda b, t: (b, t, 0)),
                  o_in(), o_in(), wide(), wide(), wide(),
                  _const_spec((WIDTH, D)), _const_spec((WIDTH, D)), _const_spec((WIDTH, D)),
                  _const_spec((D, 3 * D)), _const_spec((1, 3 * D)), _const_spec((D, D)),
                  _const_spec((1, D)), _const_spec((1, D))],
        out_specs=pl.BlockSpec((1, TQ, D), lambda b, t: (b, t, 0)),
        out_shape=jax.ShapeDtypeStruct((B, S, D), F32),
        compiler_params=_params(2),
        name="out_block",
    )(x, oa, ob, ga, gb, om,
      w_branch_mla.astype(BF16), w_branch_sb.astype(BF16), w_branch_mem.astype(BF16),
      w_merge_gate.astype(BF16), b_merge_gate.reshape(1, 3 * D), w_out.astype(BF16),
      ln_gain.reshape(1, D), ln_bias.reshape(1, D))
    return y


def kernel(x, mem, w_in, w_mem_kv, q_a_gain, w_q_b, kv_a_gain, w_kv_b, w_branch_mla, w_branch_sb,
           w_branch_mem, w_merge_gate, b_merge_gate, w_out, ln_gain, ln_bias):
    h = x
    for l in range(w_in.shape[0]):
        h = _layer(h, mem, w_in[l], w_mem_kv[l], q_a_gain[l], w_q_b[l], kv_a_gain[l], w_kv_b[l],
                   w_branch_mla[l], w_branch_sb[l], w_branch_mem[l], w_merge_gate[l], b_merge_gate[l],
                   w_out[l], ln_gain[l], ln_bias[l])
    return h
```

```python
import functools
import math

import numpy as np
import jax
import jax.numpy as jnp
from jax import lax
from jax.experimental import pallas as pl
from jax.experimental.pallas import tpu as pltpu

F32 = jnp.float32
BF16 = jnp.bfloat16

D_MODEL = 1024
MEM_LEN = 256
MLA_HEADS, MLA_NOPE, MLA_ROPE, MLA_V = 8, 64, 32, 64
MLA_Q_LORA, MLA_KV_LORA = 256, 128
SB_HEADS, SB_HEAD_DIM = 8, 64
MEM_HEADS, MEM_HEAD_DIM = 4, 128
WIDTH = 512
ROPE_BASE = 10000.0
RMS_EPS = 1e-6
LN_EPS = 1e-5
DEPTH = 1
DEEPNORM_ALPHA = (2.0 * DEPTH) ** 0.25

_OFF = np.cumsum([0, MLA_Q_LORA, MLA_KV_LORA, MLA_ROPE, WIDTH, WIDTH, WIDTH, WIDTH, WIDTH, WIDTH, WIDTH])
(O_CQ, O_CKV, O_KR, O_GA, O_QB, O_KB, O_VB, O_GB, O_QM, O_GM, O_END) = [int(v) for v in _OFF]

TQ = 256
HEAD_PAD = 128
VMEM_LIMIT = 56 * 1024 * 1024
NEG = -1e30
HEADS_PER_STEP = 4
DENSE_TILES = 4
MLA_DIAG_PAIRS_PER_TRIP = 8
MLA_FULL_PAIRS_PER_TRIP = 12
MLA_V_ROWS = 80
LOG2E = math.log2(math.e)
SB_DEAD_LOG2 = 152.0

R_QM, R_CQ, R_CKV, R_GA, R_QB, R_VB, R_GB, R_GM, R_END = [
    int(v) for v in np.cumsum([0, WIDTH, MLA_Q_LORA, MLA_KV_LORA, WIDTH, WIDTH, WIDTH, WIDTH, WIDTH])]
C_CKV, C_KB, C_ROPE, C_END = 0, 128, 640, 768

_NT = (((1,), (1,)), ((), ()))
_TN = (((0,), (0,)), ((), ()))


def _sigmoid(t):
    return 1.0 / (1.0 + jnp.exp(-t))


def _silu(t):
    return t * _sigmoid(t)


def _mem_kv_kernel(mem_ref, wk_ref, wvT_ref, km_ref, vmT_ref):
    mb = mem_ref[0].astype(BF16)
    km_ref[0] = jnp.dot(mb, wk_ref[...], preferred_element_type=F32).astype(BF16)
    vmT_ref[0] = lax.dot_general(wvT_ref[...], mb, _NT, preferred_element_type=F32).astype(BF16)


def _in_proj_kernel(x_ref, wT_ref, wS_ref, qgain_ref, kvgain_col_ref, kvgain_row_ref, wqT_ref, wvT_ref,
                    wk_ref, cosq_ref, sinq_ref, cosk_ref, sink_ref, km_ref, vmT_ref,
                    qa_ref, ka_ref, va_ref, qb_ref, kb_ref, vb_ref, ga_ref, gb_ref, om_ref):
    def tile(t, carry):
        rows = pl.ds(pl.multiple_of(t * TQ, TQ), TQ)
        xb = x_ref[0, rows, :].astype(BF16)

        def proj_T(lo, hi):
            return lax.dot_general(wT_ref[lo:hi], xb, _NT, preferred_element_type=F32)

        half_m = WIDTH // 2
        qmT = [proj_T(R_QM, R_QM + half_m), proj_T(R_QM + half_m, R_CQ)]
        pA = proj_T(R_CQ, R_GA)
        gaT = proj_T(R_GA, R_QB)
        km = km_ref[0]
        vmT = vmT_ref[0]
        inv_sqrt_d = 1.0 / math.sqrt(MEM_HEAD_DIM)
        heads_m = [slice(h * MEM_HEAD_DIM, (h + 1) * MEM_HEAD_DIM) for h in range(MEM_HEADS)]
        sm = []
        for h, sl in enumerate(heads_m):
            q_h = qmT[h // 2][(h % 2) * MEM_HEAD_DIM:(h % 2 + 1) * MEM_HEAD_DIM].astype(BF16)
            sm.append(jnp.dot(km[:, sl], q_h, preferred_element_type=F32))
        pS = jnp.dot(xb, wS_ref[...], preferred_element_type=F32)
        qbT, vbT, gbT, gmT = (proj_T(lo, lo + WIDTH) for lo in (R_QB, R_VB, R_GB, R_GM))

        for h, sl in enumerate(heads_m):
            s = sm[h] * inv_sqrt_d
            e = jnp.exp(s - jnp.max(s, axis=0, keepdims=True))
            inv_l = 1.0 / jnp.sum(e, axis=0, keepdims=True)
            o = jnp.dot(vmT[sl], e.astype(BF16), preferred_element_type=F32) * inv_l
            om_ref[0, t, sl, :] = (o * _silu(gmT[sl])).astype(BF16)

        cq = pA[0:MLA_Q_LORA]
        nq = cq * lax.rsqrt(jnp.mean(cq * cq, axis=0, keepdims=True) + RMS_EPS) * qgain_ref[...]
        qaT = jnp.dot(wqT_ref[...], nq.astype(BF16), preferred_element_type=F32)
        scale = LOG2E / math.sqrt(MLA_NOPE + MLA_ROPE)
        n_nope = MLA_HEADS * MLA_NOPE
        half = MLA_ROPE // 2
        x1 = qaT[n_nope:n_nope + MLA_HEADS * half]
        x2 = qaT[n_nope + MLA_HEADS * half:]
        cq_t, sq_t = cosq_ref[t], sinq_ref[t]
        r1 = (x1 * cq_t - x2 * sq_t) * scale
        r2 = (x1 * sq_t + x2 * cq_t) * scale
        nope = qaT[:n_nope] * scale
        zpad = jnp.zeros((HEAD_PAD - MLA_NOPE - MLA_ROPE, TQ), BF16)
        for h in range(MLA_HEADS):
            qa_ref[0, h, t, 0:MLA_NOPE, :] = nope[h * MLA_NOPE:(h + 1) * MLA_NOPE].astype(BF16)
            qa_ref[0, h, t, MLA_NOPE:MLA_NOPE + half, :] = r1[h * half:(h + 1) * half].astype(BF16)
            qa_ref[0, h, t, MLA_NOPE + half:MLA_NOPE + MLA_ROPE, :] = r2[h * half:(h + 1) * half].astype(BF16)
            qa_ref[0, h, t, MLA_NOPE + MLA_ROPE:, :] = zpad

        ckvT = pA[MLA_Q_LORA:]
        nkvT = ckvT * lax.rsqrt(jnp.mean(ckvT * ckvT, axis=0, keepdims=True) + RMS_EPS) * kvgain_col_ref[...]
        vaT = jnp.dot(wvT_ref[...], nkvT.astype(BF16), preferred_element_type=F32)
        row = lax.broadcasted_iota(jnp.int32, (MLA_V_ROWS - MLA_V, TQ), 0)
        ones_row = jnp.where(row == 0, 1.0, 0.0).astype(BF16)
        for h in range(MLA_HEADS):
            va_ref[0, h, t, 0:MLA_V, :] = vaT[h * MLA_V:(h + 1) * MLA_V].astype(BF16)
            va_ref[0, h, t, MLA_V:, :] = ones_row

        ckv = pS[:, C_CKV:C_KB]
        nkv = ckv * lax.rsqrt(jnp.mean(ckv * ckv, axis=1, keepdims=True) + RMS_EPS) * kvgain_row_ref[...]
        kr = pS[:, C_ROPE:C_END]
        kpe = kr * cosk_ref[rows, :] + pltpu.roll(kr, HEAD_PAD - MLA_ROPE, 1) * sink_ref[rows, :]
        kin = jnp.concatenate([nkv.astype(BF16), kpe.astype(BF16)], axis=1)
        ka_ref[0, rows, :] = jnp.dot(kin, wk_ref[...], preferred_element_type=F32).astype(BF16)

        kb_ref[0, rows, :] = pS[:, C_KB:C_ROPE].astype(BF16)
        zhalf = jnp.zeros((HEAD_PAD - SB_HEAD_DIM, TQ), BF16)
        for h in range(SB_HEADS):
            lo = (h % 2) * SB_HEAD_DIM
            other = SB_HEAD_DIM - lo
            qb_ref[0, h, t, lo:lo + SB_HEAD_DIM, :] = qbT[h * SB_HEAD_DIM:(h + 1) * SB_HEAD_DIM].astype(BF16)
            qb_ref[0, h, t, other:other + SB_HEAD_DIM, :] = zhalf
            vb_ref[0, h, t] = vbT[h * SB_HEAD_DIM:(h + 1) * SB_HEAD_DIM].astype(BF16)

        ga_ref[0, t] = gaT
        gb_ref[0, t] = gbT
        return carry

    lax.fori_loop(0, DENSE_TILES, tile, 0)


def _mla_attn_kernel(tab_ref, q_ref, k_ref, v_ref, o_ref, sa_ref, sb_ref, m_ref, acc_ref):
    n_heads, n_tiles = q_ref.shape[1], q_ref.shape[2]
    key_idx = lax.broadcasted_iota(jnp.int32, (TQ, TQ), 0)
    qry_idx = lax.broadcasted_iota(jnp.int32, (TQ, TQ), 1)
    causal = key_idx <= qry_idx

    def sweep(row, n_visits, diagonal, pairs_per_trip):
        def fetch(dst_ref, i):
            qi, kj = tab_ref[row, i], tab_ref[row + 1, i]
            start = pl.multiple_of(kj * TQ, TQ)
            for g in range(n_heads):
                s = jnp.dot(k_ref[0, pl.ds(start, TQ), g * HEAD_PAD:(g + 1) * HEAD_PAD], q_ref[0, g, qi],
                            preferred_element_type=F32)
                dst_ref[g] = jnp.where(causal, s, NEG) if diagonal else s

        def consume(src_ref, i):
            qi, kj = tab_ref[row, i], tab_ref[row + 1, i]
            for g in range(n_heads):
                slot = qi * n_heads + g
                s = src_ref[g]
                m_new = jnp.max(s, axis=0, keepdims=True)
                if not diagonal:
                    m_old = m_ref[slot]
                    m_new = jnp.maximum(m_old, m_new)
                pv = jnp.dot(v_ref[0, g, kj], jnp.exp2(s - m_new).astype(BF16),
                             preferred_element_type=F32)
                acc_ref[slot] = pv if diagonal else jnp.exp2(m_old - m_new) * acc_ref[slot] + pv
                m_ref[slot] = m_new

        def pair(p):
            fetch(sb_ref, 2 * p + 1)
            consume(sa_ref, 2 * p)
            fetch(sa_ref, 2 * p + 2)
            consume(sb_ref, 2 * p + 1)

        def trip(t, carry):
            for j in range(pairs_per_trip):
                pair(pairs_per_trip * t + j)
            return carry

        assert n_visits % (2 * pairs_per_trip) == 0
        fetch(sa_ref, 0)
        lax.fori_loop(0, n_visits // (2 * pairs_per_trip), trip, 0)

    sweep(0, n_tiles, True, MLA_DIAG_PAIRS_PER_TRIP)
    sweep(2, n_tiles * (n_tiles - 1) // 2, False, MLA_FULL_PAIRS_PER_TRIP)

    def finish(qi, carry):
        for g in range(n_heads):
            acc = acc_ref[qi * n_heads + g]
            o_ref[0, g, qi] = acc[0:MLA_V] * (1.0 / acc[MLA_V:MLA_V + 1])
        return carry

    lax.fori_loop(0, n_tiles, finish, 0)


def _sb_attn_kernel(q_ref, k_ref, v_ref, tri_ref, o_ref):
    n_heads, n_tiles = q_ref.shape[1], q_ref.shape[2]
    key_idx = lax.broadcasted_iota(jnp.int32, (TQ, TQ), 0)
    qry_idx = lax.broadcasted_iota(jnp.int32, (TQ, TQ), 1)
    strict = key_idx < qry_idx

    def scores(chains, kjs):
        starts = [pl.multiple_of(kj * TQ, TQ) for kj in kjs]
        us = []
        for g, slot, qT in chains:
            blk = (g // 2) * HEAD_PAD
            us.append(jnp.dot(k_ref[0, pl.ds(starts[slot], TQ), blk:blk + HEAD_PAD], qT,
                              preferred_element_type=F32))
        return us

    def suffix_sums(us, masked):
        suffixes = []
        for u in us:
            sp2 = jnp.maximum(u, 0.0) + jnp.log(1.0 + jnp.exp2(-jnp.abs(u))) * LOG2E
            if masked:
                sp2 = jnp.where(strict, sp2, 0.0)
            hi = sp2.astype(BF16)
            lo = (sp2 - hi.astype(F32)).astype(BF16)
            suffixes.append(jnp.dot(tri_ref[...], jnp.concatenate([hi, lo], axis=0),
                                    preferred_element_type=F32))
        return suffixes

    def accumulate(chains, q_tiles, kjs, us, suffixes, runs, masked):
        new = []
        for c, (g, slot, _) in enumerate(chains):
            arg = us[c] - suffixes[c]
            if masked:
                arg = jnp.where(strict, arg, NEG)
            pv = jnp.dot(v_ref[0, g, kjs[slot]], jnp.exp2(arg).astype(BF16), preferred_element_type=F32)
            if not masked:
                pv = pv * jnp.exp2(-runs[c])
            o_ref[0, g, q_tiles[slot]] = pv if masked else o_ref[0, g, q_tiles[slot]] + pv
            new.append(runs[c] + suffixes[c][0:1, :])
        return tuple(new)

    def visit(chains, q_tiles, kjs, runs, masked):
        us = scores(chains, kjs)
        return accumulate(chains, q_tiles, kjs, us, suffix_sums(us, masked), runs, masked)

    def all_dead(runs):
        run_min = runs[0]
        for run in runs[1:]:
            run_min = jnp.minimum(run_min, run)
        return jnp.min(run_min) > SB_DEAD_LOG2

    def pair_body(j, carry):
        q_tiles = (2 * j, 2 * j + 1)
        chains = [(g, slot, q_ref[0, g, q_tiles[slot]]) for slot in range(2) for g in range(n_heads)]

        def key_tiles(i):
            return jnp.maximum(q_tiles[0] - i, 0), q_tiles[1] - i

        def spend(i, runs):
            spent = i > q_tiles[0]
            return tuple(run if slot else jnp.where(spent, -NEG, run)
                         for (_, slot, _), run in zip(chains, runs))

        us0, us1 = scores(chains, q_tiles), scores(chains, key_tiles(1))
        runs = accumulate(chains, q_tiles, q_tiles, us0, suffix_sums(us0, True),
                          tuple(jnp.zeros((1, TQ), F32) for _ in chains), True)
        runs = accumulate(chains, q_tiles, key_tiles(1), us1, suffix_sums(us1, False), spend(1, runs), False)

        def cond(c):
            i, dead, _ = c
            return jnp.logical_and(i <= q_tiles[1], jnp.logical_not(dead))

        def body(c):
            i, _, runs = c
            new = visit(chains, q_tiles, key_tiles(i), spend(i, runs), False)
            return i + 1, all_dead(new), new

        lax.while_loop(cond, body, (jnp.int32(2), all_dead(runs), runs))
        return carry

    lax.fori_loop(0, n_tiles // 2, pair_body, 0)


def _out_block_kernel(x_ref, oa_ref, ob_ref, ga_ref, gb_ref, om_ref, wa_ref, wb_ref, wm_ref,
                      wg_ref, bg_ref, wout_ref, lng_ref, lnb_ref, y_ref):
    def tile(t, carry):
        rows = pl.ds(pl.multiple_of(t * TQ, TQ), TQ)
        x = x_ref[0, rows, :]
        xb = x.astype(BF16)
        oa = oa_ref[0, :, t].reshape(WIDTH, TQ)
        ob = ob_ref[0, :, t].reshape(WIDTH, TQ)
        hs = [(oa * _silu(ga_ref[0, t])).astype(BF16), (ob * _silu(gb_ref[0, t])).astype(BF16), om_ref[0, t]]
        merged = None
        for j, (w_ref, h) in enumerate(zip((wa_ref, wb_ref, wm_ref), hs)):
            cols = slice(j * D_MODEL, (j + 1) * D_MODEL)
            z = jnp.dot(xb, wg_ref[:, cols], preferred_element_type=F32) + bg_ref[:, cols]
            term = _sigmoid(z) * lax.dot_general(h, w_ref[...], _TN, preferred_element_type=F32)
            merged = term if merged is None else merged + term
        out = jnp.dot(merged.astype(BF16), wout_ref[...], preferred_element_type=F32)
        r = DEEPNORM_ALPHA * x + out
        mu = jnp.mean(r, axis=1, keepdims=True)
        rc = r - mu
        var = jnp.mean(rc * rc, axis=1, keepdims=True)
        y_ref[0, rows, :] = rc * lax.rsqrt(var + LN_EPS) * lng_ref[...] + lnb_ref[...]
        return carry

    lax.fori_loop(0, DENSE_TILES, tile, 0)


def _const_spec(shape):
    return pl.BlockSpec(shape, lambda *_: (0,) * len(shape))


def _params(n_axes):
    return pltpu.CompilerParams(dimension_semantics=("parallel",) * n_axes, vmem_limit_bytes=VMEM_LIMIT)


def _layer(x, mem, w_in, w_mem_kv, q_a_gain, w_q_b, kv_a_gain, w_kv_b, w_branch_mla, w_branch_sb,
           w_branch_mem, w_merge_gate, b_merge_gate, w_out, ln_gain, ln_bias):
    B, S, D = x.shape
    assert D == D_MODEL and S % (2 * TQ) == 0 and mem.shape == (B, MEM_LEN, D)
    NT = S // TQ
    half = MLA_ROPE // 2

    sb_scale = LOG2E / math.sqrt(SB_HEAD_DIM)
    wT = jnp.concatenate([w_in[:, O_QM:O_GM], w_in[:, O_CQ:O_KR], w_in[:, O_GA:O_QB],
                          w_in[:, O_QB:O_KB] * sb_scale, w_in[:, O_VB:O_QM], w_in[:, O_GM:O_END]],
                         axis=1).astype(BF16).T
    w_rope = w_in[:, O_KR:O_GA]
    w_rot = jnp.concatenate([-w_rope[:, half:], w_rope[:, :half]], axis=1)
    wS = jnp.concatenate([w_in[:, O_CKV:O_KR], w_in[:, O_KB:O_VB], w_rope, w_rot,
                          jnp.zeros((D, HEAD_PAD - 2 * MLA_ROPE), F32)], axis=1).astype(BF16)

    qd = MLA_NOPE + MLA_ROPE
    wq3 = w_q_b.reshape(MLA_Q_LORA, MLA_HEADS, qd)
    wqT = jnp.concatenate([wq3[:, :, :MLA_NOPE].reshape(MLA_Q_LORA, -1),
                           wq3[:, :, MLA_NOPE:MLA_NOPE + half].reshape(MLA_Q_LORA, -1),
                           wq3[:, :, MLA_NOPE + half:].reshape(MLA_Q_LORA, -1)], axis=1).astype(BF16).T
    kvd = MLA_NOPE + MLA_V
    wvT = w_kv_b.reshape(MLA_KV_LORA, MLA_HEADS, kvd)[:, :, MLA_NOPE:].reshape(MLA_KV_LORA, -1).astype(BF16).T
    knope_mask = np.zeros((1, MLA_HEADS * kvd), np.float32)
    place = np.zeros((HEAD_PAD, MLA_HEADS * HEAD_PAD), np.float32)
    for h in range(MLA_HEADS):
        knope_mask[0, h * kvd:h * kvd + MLA_NOPE] = 1.0
        place[np.arange(MLA_ROPE), h * HEAD_PAD + MLA_NOPE + np.arange(MLA_ROPE)] = 1.0
    wk = jnp.concatenate([w_kv_b * knope_mask, jnp.asarray(place)], axis=0).astype(BF16)

    qgain = q_a_gain.reshape(MLA_Q_LORA, 1)
    kvgain_col = kv_a_gain.reshape(MLA_KV_LORA, 1)
    kvgain_row = kv_a_gain.reshape(1, MLA_KV_LORA)

    freqs = ROPE_BASE ** (-jnp.arange(half, dtype=F32) / half)
    ang = jnp.arange(S, dtype=jnp.int32).astype(F32)[:, None] * freqs[None, :]
    cos, sin = jnp.cos(ang), jnp.sin(ang)
    tiles_q = lambda a: jnp.tile(a.reshape(NT, TQ, half).transpose(0, 2, 1), (1, MLA_HEADS, 1))
    cosq, sinq = tiles_q(cos), tiles_q(sin)
    zeros_k = jnp.zeros((S, HEAD_PAD - MLA_ROPE), F32)
    cosk = jnp.concatenate([cos, cos, zeros_k], axis=1)
    sink = jnp.concatenate([sin, sin, zeros_k], axis=1)

    wkm = w_mem_kv[:, :WIDTH].astype(BF16)
    wvmT = w_mem_kv[:, WIDTH:].T.astype(BF16)
    km, vmT = pl.pallas_call(
        _mem_kv_kernel,
        grid=(B,),
        in_specs=[pl.BlockSpec((1, MEM_LEN, D), lambda b: (b, 0, 0)),
                  _const_spec((D, WIDTH)), _const_spec((WIDTH, D))],
        out_specs=[pl.BlockSpec((1, MEM_LEN, WIDTH), lambda b: (b, 0, 0)),
                   pl.BlockSpec((1, WIDTH, MEM_LEN), lambda b: (b, 0, 0))],
        out_shape=[jax.ShapeDtypeStruct((B, MEM_LEN, WIDTH), BF16),
                   jax.ShapeDtypeStruct((B, WIDTH, MEM_LEN), BF16)],
        compiler_params=_params(1),
        name="mem_kv",
    )(mem, wkm, wvmT)

    TD = DENSE_TILES
    assert NT % TD == 0
    head_q = lambda: pl.BlockSpec((1, MLA_HEADS, TD, HEAD_PAD, TQ), lambda b, t: (b, 0, t, 0, 0))
    head_v = lambda rows: pl.BlockSpec((1, MLA_HEADS, TD, rows, TQ), lambda b, t: (b, 0, t, 0, 0))
    wide = lambda: pl.BlockSpec((1, TD, WIDTH, TQ), lambda b, t: (b, t, 0, 0))
    qa, ka, va, qb, kb, vb, ga, gb, om = pl.pallas_call(
        _in_proj_kernel,
        grid=(B, NT // TD),
        in_specs=[pl.BlockSpec((1, TD * TQ, D), lambda b, t: (b, t, 0)),
                  _const_spec((R_END, D)), _const_spec((D, C_END)),
                  _const_spec((MLA_Q_LORA, 1)), _const_spec((MLA_KV_LORA, 1)), _const_spec((1, MLA_KV_LORA)),
                  _const_spec((MLA_HEADS * qd, MLA_Q_LORA)), _const_spec((WIDTH, MLA_KV_LORA)),
                  _const_spec((2 * HEAD_PAD, MLA_HEADS * HEAD_PAD)),
                  pl.BlockSpec((TD, MLA_HEADS * half, TQ), lambda b, t: (t, 0, 0)),
                  pl.BlockSpec((TD, MLA_HEADS * half, TQ), lambda b, t: (t, 0, 0)),
                  pl.BlockSpec((TD * TQ, HEAD_PAD), lambda b, t: (t, 0)),
                  pl.BlockSpec((TD * TQ, HEAD_PAD), lambda b, t: (t, 0)),
                  pl.BlockSpec((1, MEM_LEN, WIDTH), lambda b, t: (b, 0, 0)),
                  pl.BlockSpec((1, WIDTH, MEM_LEN), lambda b, t: (b, 0, 0))],
        out_specs=[head_q(),
                   pl.BlockSpec((1, TD * TQ, MLA_HEADS * HEAD_PAD), lambda b, t: (b, t, 0)),
                   head_v(MLA_V_ROWS),
                   head_q(),
                   pl.BlockSpec((1, TD * TQ, WIDTH), lambda b, t: (b, t, 0)),
                   head_v(SB_HEAD_DIM),
                   wide(), wide(), wide()],
        out_shape=[jax.ShapeDtypeStruct((B, MLA_HEADS, NT, HEAD_PAD, TQ), BF16),
                   jax.ShapeDtypeStruct((B, S, MLA_HEADS * HEAD_PAD), BF16),
                   jax.ShapeDtypeStruct((B, MLA_HEADS, NT, MLA_V_ROWS, TQ), BF16),
                   jax.ShapeDtypeStruct((B, SB_HEADS, NT, HEAD_PAD, TQ), BF16),
                   jax.ShapeDtypeStruct((B, S, WIDTH), BF16),
                   jax.ShapeDtypeStruct((B, SB_HEADS, NT, SB_HEAD_DIM, TQ), BF16),
                   jax.ShapeDtypeStruct((B, NT, WIDTH, TQ), F32),
                   jax.ShapeDtypeStruct((B, NT, WIDTH, TQ), F32),
                   jax.ShapeDtypeStruct((B, NT, WIDTH, TQ), BF16)],
        compiler_params=_params(2),
        name="in_proj",
    )(x, wT, wS, qgain, kvgain_col, kvgain_row, wqT, wvT, wk, cosq, sinq, cosk, sink, km, vmT)

    G = HEADS_PER_STEP
    group = lambda rows: pl.BlockSpec((1, G, NT, rows, TQ), lambda b, h: (b, h, 0, 0, 0))
    q_spec, v_spec = group(HEAD_PAD), group(MLA_V)
    o_shape = jax.ShapeDtypeStruct((B, MLA_HEADS, NT, MLA_V, TQ), F32)
    full = [(qi, kj) for qi in range(NT) for kj in range(qi)]
    tab_np = np.zeros((4, len(full) + 1), np.int32)
    tab_np[0:2, :] = NT - 1
    tab_np[0:2, :NT] = np.arange(NT)
    tab_np[2:4, :] = np.array(full[-1])[:, None]
    tab_np[2:4, :len(full)] = np.array(full).T
    oa = pl.pallas_call(
        _mla_attn_kernel,
        grid=(B, MLA_HEADS // G),
        in_specs=[pl.BlockSpec(memory_space=pltpu.SMEM),
                  q_spec, pl.BlockSpec((1, S, G * HEAD_PAD), lambda b, h: (b, 0, h)), group(MLA_V_ROWS)],
        out_specs=v_spec,
        out_shape=o_shape,
        scratch_shapes=[pltpu.VMEM((G, TQ, TQ), F32), pltpu.VMEM((G, TQ, TQ), F32),
                        pltpu.VMEM((NT * G, 1, TQ), F32), pltpu.VMEM((NT * G, MLA_V_ROWS, TQ), F32)],
        compiler_params=_params(2),
        name="mla_attn",
    )(jnp.asarray(tab_np), qa, ka, va)

    tri_np = np.triu(np.ones((TQ, TQ), np.float32))
    tri = jnp.asarray(np.concatenate([tri_np, tri_np], axis=1), dtype=BF16)
    ob = pl.pallas_call(
        _sb_attn_kernel,
        grid=(B, SB_HEADS // G),
        in_specs=[q_spec, pl.BlockSpec((1, S, G * SB_HEAD_DIM), lambda b, h: (b, 0, h)), v_spec,
                  _const_spec((TQ, 2 * TQ))],
        out_specs=v_spec,
        out_shape=o_shape,
        compiler_params=_params(2),
        name="sb_attn",
    )(qb, kb, vb, tri)

    TD = DENSE_TILES
    assert NT % TD == 0
    o_in = lambda: pl.BlockSpec((1, MLA_HEADS, TD, MLA_V, TQ), lambda b, t: (b, 0, t, 0, 0))
    wide_d = lambda: pl.BlockSpec((1, TD, WIDTH, TQ), lambda b, t: (b, t, 0, 0))
    y = pl.pallas_call(
        _out_block_kernel,
        grid=(B, NT // TD),
        in_specs=[pl.BlockSpec((1, TD * TQ, D), lambda b, t: (b, t, 0)),
                  o_in(), o_in(), wide_d(), wide_d(), wide_d(),
                  _const_spec((WIDTH, D)), _const_spec((WIDTH, D)), _const_spec((WIDTH, D)),
                  _const_spec((D, 3 * D)), _const_spec((1, 3 * D)), _const_spec((D, D)),
                  _const_spec((1, D)), _const_spec((1, D))],
        out_specs=pl.BlockSpec((1, TD * TQ, D), lambda b, t: (b, t, 0)),
        out_shape=jax.ShapeDtypeStruct((B, S, D), F32),
        compiler_params=_params(2),
        name="out_block",
    )(x, oa, ob, ga, gb, om,
      w_branch_mla.astype(BF16), w_branch_sb.astype(BF16), w_branch_mem.astype(BF16),
      w_merge_gate.astype(BF16), b_merge_gate.reshape(1, 3 * D), w_out.astype(BF16),
      ln_gain.reshape(1, D), ln_bias.reshape(1, D))
    return y


def kernel(x, mem, w_in, w_mem_kv, q_a_gain, w_q_b, kv_a_gain, w_kv_b, w_branch_mla, w_branch_sb,
           w_branch_mem, w_merge_gate, b_merge_gate, w_out, ln_gain, ln_bias):
    h = x
    for l in range(w_in.shape[0]):
        h = _layer(h, mem, w_in[l], w_mem_kv[l], q_a_gain[l], w_q_b[l], kv_a_gain[l], w_kv_b[l],
                   w_branch_mla[l], w_branch_sb[l], w_branch_mem[l], w_merge_gate[l], b_merge_gate[l],
                   w_out[l], ln_gain[l], ln_bias[l])
    return h
```

```python
import math

import numpy as np
import jax
import jax.numpy as jnp
from jax import lax
from jax.experimental import pallas as pl
from jax.experimental.pallas import tpu as pltpu

F32 = jnp.float32
BF16 = jnp.bfloat16

D_MODEL = 1024
MEM_LEN = 256
MLA_HEADS, MLA_NOPE, MLA_ROPE, MLA_V = 8, 64, 32, 64
MLA_Q_LORA, MLA_KV_LORA = 256, 128
SB_HEADS, SB_HEAD_DIM = 8, 64
MEM_HEADS, MEM_HEAD_DIM = 4, 128
WIDTH = 512
ROPE_BASE = 10000.0
RMS_EPS = 1e-6
LN_EPS = 1e-5
DEPTH = 1
DEEPNORM_ALPHA = (2.0 * DEPTH) ** 0.25

_OFF = np.cumsum([0, MLA_Q_LORA, MLA_KV_LORA, MLA_ROPE, WIDTH, WIDTH, WIDTH, WIDTH, WIDTH, WIDTH, WIDTH])
(O_CQ, O_CKV, O_KR, O_GA, O_QB, O_KB, O_VB, O_GB, O_QM, O_GM, O_END) = [int(v) for v in _OFF]

TQ = 256
HEAD_PAD = 128
VMEM_LIMIT = 56 * 1024 * 1024
NEG = -1e30
HEADS_PER_STEP = 4
DENSE_TILES = 4
MLA_DIAG_PAIRS_PER_TRIP = 8
MLA_FULL_PAIRS_PER_TRIP = 12
MLA_V_ROWS = 80
LOG2E = math.log2(math.e)
SB_DEAD_LOG2 = 152.0

R_QM, R_CQ, R_CKV, R_GA, R_QB, R_VB, R_GB, R_GM, R_END = [
    int(v) for v in np.cumsum([0, WIDTH, MLA_Q_LORA, MLA_KV_LORA, WIDTH, WIDTH, WIDTH, WIDTH, WIDTH])]
C_CKV, C_KB, C_ROPE, C_END = 0, 128, 640, 768

_NT = (((1,), (1,)), ((), ()))
_TN = (((0,), (0,)), ((), ()))


def _sigmoid(t):
    return 1.0 / (1.0 + jnp.exp(-t))


def _silu(t):
    return t * _sigmoid(t)


def _mem_kv_kernel(mem_ref, wk_ref, wvT_ref, km_ref, vmT_ref):
    mb = mem_ref[0].astype(BF16)
    km_ref[0] = jnp.dot(mb, wk_ref[...], preferred_element_type=F32).astype(BF16)
    vmT_ref[0] = lax.dot_general(wvT_ref[...], mb, _NT, preferred_element_type=F32).astype(BF16)


def _in_proj_kernel(x_ref, wT_ref, wS_ref, qgain_ref, kvgain_col_ref, kvgain_row_ref, wqT_ref, wvT_ref,
                    wk_ref, cosq_ref, sinq_ref, cosk_ref, sink_ref, km_ref, vmT_ref,
                    qa_ref, ka_ref, va_ref, qb_ref, kb_ref, vb_ref, ga_ref, gb_ref, om_ref):
    def tile(t, carry):
        rows = pl.ds(pl.multiple_of(t * TQ, TQ), TQ)
        xb = x_ref[0, rows, :].astype(BF16)

        def proj_T(lo, hi):
            return lax.dot_general(wT_ref[lo:hi], xb, _NT, preferred_element_type=F32)

        half_m = WIDTH // 2
        qmT = [proj_T(R_QM, R_QM + half_m), proj_T(R_QM + half_m, R_CQ)]
        pA = proj_T(R_CQ, R_GA)
        gaT = proj_T(R_GA, R_QB)
        km = km_ref[0]
        vmT = vmT_ref[0]
        inv_sqrt_d = 1.0 / math.sqrt(MEM_HEAD_DIM)
        heads_m = [slice(h * MEM_HEAD_DIM, (h + 1) * MEM_HEAD_DIM) for h in range(MEM_HEADS)]
        sm = []
        for h, sl in enumerate(heads_m):
            q_h = qmT[h // 2][(h % 2) * MEM_HEAD_DIM:(h % 2 + 1) * MEM_HEAD_DIM].astype(BF16)
            sm.append(jnp.dot(km[:, sl], q_h, preferred_element_type=F32))
        pS = jnp.dot(xb, wS_ref[...], preferred_element_type=F32)
        qbT, vbT, gbT, gmT = (proj_T(lo, lo + WIDTH) for lo in (R_QB, R_VB, R_GB, R_GM))

        for h, sl in enumerate(heads_m):
            s = sm[h] * inv_sqrt_d
            e = jnp.exp(s - jnp.max(s, axis=0, keepdims=True))
            inv_l = 1.0 / jnp.sum(e, axis=0, keepdims=True)
            o = jnp.dot(vmT[sl], e.astype(BF16), preferred_element_type=F32) * inv_l
            om_ref[0, t, sl, :] = (o * _silu(gmT[sl])).astype(BF16)

        cq = pA[0:MLA_Q_LORA]
        nq = cq * lax.rsqrt(jnp.mean(cq * cq, axis=0, keepdims=True) + RMS_EPS) * qgain_ref[...]
        qaT = jnp.dot(wqT_ref[...], nq.astype(BF16), preferred_element_type=F32)
        scale = LOG2E / math.sqrt(MLA_NOPE + MLA_ROPE)
        n_nope = MLA_HEADS * MLA_NOPE
        half = MLA_ROPE // 2
        x1 = qaT[n_nope:n_nope + MLA_HEADS * half]
        x2 = qaT[n_nope + MLA_HEADS * half:]
        cq_t, sq_t = cosq_ref[t], sinq_ref[t]
        r1 = (x1 * cq_t - x2 * sq_t) * scale
        r2 = (x1 * sq_t + x2 * cq_t) * scale
        nope = qaT[:n_nope] * scale
        zpad = jnp.zeros((HEAD_PAD - MLA_NOPE - MLA_ROPE, TQ), BF16)
        for h in range(MLA_HEADS):
            qa_ref[0, h, t, 0:MLA_NOPE, :] = nope[h * MLA_NOPE:(h + 1) * MLA_NOPE].astype(BF16)
            qa_ref[0, h, t, MLA_NOPE:MLA_NOPE + half, :] = r1[h * half:(h + 1) * half].astype(BF16)
            qa_ref[0, h, t, MLA_NOPE + half:MLA_NOPE + MLA_ROPE, :] = r2[h * half:(h + 1) * half].astype(BF16)
            qa_ref[0, h, t, MLA_NOPE + MLA_ROPE:, :] = zpad

        ckvT = pA[MLA_Q_LORA:]
        nkvT = ckvT * lax.rsqrt(jnp.mean(ckvT * ckvT, axis=0, keepdims=True) + RMS_EPS) * kvgain_col_ref[...]
        vaT = jnp.dot(wvT_ref[...], nkvT.astype(BF16), preferred_element_type=F32)
        row = lax.broadcasted_iota(jnp.int32, (MLA_V_ROWS - MLA_V, TQ), 0)
        ones_row = jnp.where(row == 0, 1.0, 0.0).astype(BF16)
        for h in range(MLA_HEADS):
            va_ref[0, h, t, 0:MLA_V, :] = vaT[h * MLA_V:(h + 1) * MLA_V].astype(BF16)
            va_ref[0, h, t, MLA_V:, :] = ones_row

        ckv = pS[:, C_CKV:C_KB]
        nkv = ckv * lax.rsqrt(jnp.mean(ckv * ckv, axis=1, keepdims=True) + RMS_EPS) * kvgain_row_ref[...]
        kr = pS[:, C_ROPE:C_END]
        kpe = kr * cosk_ref[rows, :] + pltpu.roll(kr, HEAD_PAD - MLA_ROPE, 1) * sink_ref[rows, :]
        kin = jnp.concatenate([nkv.astype(BF16), kpe.astype(BF16)], axis=1)
        ka_ref[0, rows, :] = jnp.dot(kin, wk_ref[...], preferred_element_type=F32).astype(BF16)

        kb_ref[0, rows, :] = pS[:, C_KB:C_ROPE].astype(BF16)
        zhalf = jnp.zeros((HEAD_PAD - SB_HEAD_DIM, TQ), BF16)
        for h in range(SB_HEADS):
            lo = (h % 2) * SB_HEAD_DIM
            other = SB_HEAD_DIM - lo
            qb_ref[0, h, t, lo:lo + SB_HEAD_DIM, :] = qbT[h * SB_HEAD_DIM:(h + 1) * SB_HEAD_DIM].astype(BF16)
            qb_ref[0, h, t, other:other + SB_HEAD_DIM, :] = zhalf
            vb_ref[0, h, t] = vbT[h * SB_HEAD_DIM:(h + 1) * SB_HEAD_DIM].astype(BF16)

        ga_ref[0, t] = gaT
        gb_ref[0, t] = gbT
        return carry

    lax.fori_loop(0, DENSE_TILES, tile, 0)


def _mla_attn_kernel(tab_ref, q_ref, k_ref, v_ref, o_ref, sa_ref, sb_ref, m_ref, acc_ref):
    n_heads, n_tiles = q_ref.shape[1], q_ref.shape[2]
    key_idx = lax.broadcasted_iota(jnp.int32, (TQ, TQ), 0)
    qry_idx = lax.broadcasted_iota(jnp.int32, (TQ, TQ), 1)
    causal = key_idx <= qry_idx

    def sweep(row, n_visits, diagonal, pairs_per_trip):
        def fetch(dst_ref, i):
            qi, kj = tab_ref[row, i], tab_ref[row + 1, i]
            start = pl.multiple_of(kj * TQ, TQ)
            for g in range(n_heads):
                s = jnp.dot(k_ref[0, pl.ds(start, TQ), g * HEAD_PAD:(g + 1) * HEAD_PAD], q_ref[0, g, qi],
                            preferred_element_type=F32)
                dst_ref[g] = jnp.where(causal, s, NEG) if diagonal else s

        def consume(src_ref, i):
            qi, kj = tab_ref[row, i], tab_ref[row + 1, i]
            for g in range(n_heads):
                slot = qi * n_heads + g
                s = src_ref[g]
                m_new = jnp.max(s, axis=0, keepdims=True)
                if not diagonal:
                    m_old = m_ref[slot]
                    m_new = jnp.maximum(m_old, m_new)
                pv = jnp.dot(v_ref[0, g, kj], jnp.exp2(s - m_new).astype(BF16),
                             preferred_element_type=F32)
                acc_ref[slot] = pv if diagonal else jnp.exp2(m_old - m_new) * acc_ref[slot] + pv
                m_ref[slot] = m_new

        def pair(p):
            fetch(sb_ref, 2 * p + 1)
            consume(sa_ref, 2 * p)
            fetch(sa_ref, 2 * p + 2)
            consume(sb_ref, 2 * p + 1)

        def trip(t, carry):
            for j in range(pairs_per_trip):
                pair(pairs_per_trip * t + j)
            return carry

        assert n_visits % (2 * pairs_per_trip) == 0
        fetch(sa_ref, 0)
        lax.fori_loop(0, n_visits // (2 * pairs_per_trip), trip, 0)

    sweep(0, n_tiles, True, MLA_DIAG_PAIRS_PER_TRIP)
    sweep(2, n_tiles * (n_tiles - 1) // 2, False, MLA_FULL_PAIRS_PER_TRIP)

    def finish(qi, carry):
        for g in range(n_heads):
            acc = acc_ref[qi * n_heads + g]
            o_ref[0, g, qi] = acc[0:MLA_V] * (1.0 / acc[MLA_V:MLA_V + 1])
        return carry

    lax.fori_loop(0, n_tiles, finish, 0)


def _sb_attn_kernel(q_ref, k_ref, v_ref, tri_ref, o_ref):
    n_heads, n_tiles = q_ref.shape[1], q_ref.shape[2]
    key_idx = lax.broadcasted_iota(jnp.int32, (TQ, TQ), 0)
    qry_idx = lax.broadcasted_iota(jnp.int32, (TQ, TQ), 1)
    strict = key_idx < qry_idx

    def scores(chains, kjs):
        starts = [pl.multiple_of(kj * TQ, TQ) for kj in kjs]
        us = []
        for g, slot, qT in chains:
            blk = (g // 2) * HEAD_PAD
            us.append(jnp.dot(k_ref[0, pl.ds(starts[slot], TQ), blk:blk + HEAD_PAD], qT,
                              preferred_element_type=F32))
        return us

    def suffix_sums(us, masked):
        suffixes = []
        for u in us:
            sp2 = jnp.maximum(u, 0.0) + jnp.log(1.0 + jnp.exp2(-jnp.abs(u))) * LOG2E
            if masked:
                sp2 = jnp.where(strict, sp2, 0.0)
            hi = sp2.astype(BF16)
            lo = (sp2 - hi.astype(F32)).astype(BF16)
            suffixes.append(jnp.dot(tri_ref[...], jnp.concatenate([hi, lo], axis=0),
                                    preferred_element_type=F32))
        return suffixes

    def accumulate(chains, q_tiles, kjs, us, suffixes, runs, masked):
        new = []
        for c, (g, slot, _) in enumerate(chains):
            arg = us[c] - suffixes[c]
            if masked:
                arg = jnp.where(strict, arg, NEG)
            pv = jnp.dot(v_ref[0, g, kjs[slot]], jnp.exp2(arg).astype(BF16), preferred_element_type=F32)
            if not masked:
                pv = pv * jnp.exp2(-runs[c])
            o_ref[0, g, q_tiles[slot]] = pv if masked else o_ref[0, g, q_tiles[slot]] + pv
            new.append(runs[c] + suffixes[c][0:1, :])
        return tuple(new)

    def visit(chains, q_tiles, kjs, runs, masked):
        us = scores(chains, kjs)
        return accumulate(chains, q_tiles, kjs, us, suffix_sums(us, masked), runs, masked)

    def all_dead(runs):
        run_min = runs[0]
        for run in runs[1:]:
            run_min = jnp.minimum(run_min, run)
        return jnp.min(run_min) > SB_DEAD_LOG2

    def pair_body(j, carry):
        q_tiles = (2 * j, 2 * j + 1)
        chains = [(g, slot, q_ref[0, g, q_tiles[slot]]) for slot in range(2) for g in range(n_heads)]

        def key_tiles(i):
            return jnp.maximum(q_tiles[0] - i, 0), q_tiles[1] - i

        def spend(i, runs):
            spent = i > q_tiles[0]
            return tuple(run if slot else jnp.where(spent, -NEG, run)
                         for (_, slot, _), run in zip(chains, runs))

        us0, us1 = scores(chains, q_tiles), scores(chains, key_tiles(1))
        runs = accumulate(chains, q_tiles, q_tiles, us0, suffix_sums(us0, True),
                          tuple(jnp.zeros((1, TQ), F32) for _ in chains), True)
        runs = accumulate(chains, q_tiles, key_tiles(1), us1, suffix_sums(us1, False), spend(1, runs), False)

        def cond(c):
            i, dead, _ = c
            return jnp.logical_and(i <= q_tiles[1], jnp.logical_not(dead))

        def body(c):
            i, _, runs = c
            new = visit(chains, q_tiles, key_tiles(i), spend(i, runs), False)
            return i + 1, all_dead(new), new

        lax.while_loop(cond, body, (jnp.int32(2), all_dead(runs), runs))
        return carry

    lax.fori_loop(0, n_tiles // 2, pair_body, 0)


def _out_block_kernel(x_ref, oa_ref, ob_ref, ga_ref, gb_ref, om_ref, wa_ref, wb_ref, wm_ref,
                      wg_ref, bg_ref, wout_ref, lng_ref, lnb_ref, y_ref):
    def tile(t, carry):
        rows = pl.ds(pl.multiple_of(t * TQ, TQ), TQ)
        x = x_ref[0, rows, :]
        xb = x.astype(BF16)
        oa = oa_ref[0, :, t].reshape(WIDTH, TQ)
        ob = ob_ref[0, :, t].reshape(WIDTH, TQ)
        hs = [(oa * _silu(ga_ref[0, t])).astype(BF16), (ob * _silu(gb_ref[0, t])).astype(BF16), om_ref[0, t]]
        merged = None
        for j, (w_ref, h) in enumerate(zip((wa_ref, wb_ref, wm_ref), hs)):
            cols = slice(j * D_MODEL, (j + 1) * D_MODEL)
            z = jnp.dot(xb, wg_ref[:, cols], preferred_element_type=F32) + bg_ref[:, cols]
            term = _sigmoid(z) * lax.dot_general(h, w_ref[...], _TN, preferred_element_type=F32)
            merged = term if merged is None else merged + term
        out = jnp.dot(merged.astype(BF16), wout_ref[...], preferred_element_type=F32)
        r = DEEPNORM_ALPHA * x + out
        mu = jnp.mean(r, axis=1, keepdims=True)
        rc = r - mu
        var = jnp.mean(rc * rc, axis=1, keepdims=True)
        y_ref[0, rows, :] = rc * lax.rsqrt(var + LN_EPS) * lng_ref[...] + lnb_ref[...]
        return carry

    lax.fori_loop(0, DENSE_TILES, tile, 0)


def _const_spec(shape):
    return pl.BlockSpec(shape, lambda *_: (0,) * len(shape))


def _params(n_axes):
    return pltpu.CompilerParams(dimension_semantics=("parallel",) * n_axes, vmem_limit_bytes=VMEM_LIMIT)


def _layer(x, mem, w_in, w_mem_kv, q_a_gain, w_q_b, kv_a_gain, w_kv_b, w_branch_mla, w_branch_sb,
           w_branch_mem, w_merge_gate, b_merge_gate, w_out, ln_gain, ln_bias):
    B, S, D = x.shape
    assert D == D_MODEL and S % (2 * TQ) == 0 and mem.shape == (B, MEM_LEN, D)
    NT = S // TQ
    half = MLA_ROPE // 2

    sb_scale = LOG2E / math.sqrt(SB_HEAD_DIM)
    wT = jnp.concatenate([w_in[:, O_QM:O_GM], w_in[:, O_CQ:O_KR], w_in[:, O_GA:O_QB],
                          w_in[:, O_QB:O_KB] * sb_scale, w_in[:, O_VB:O_QM], w_in[:, O_GM:O_END]],
                         axis=1).astype(BF16).T
    w_rope = w_in[:, O_KR:O_GA]
    w_rot = jnp.concatenate([-w_rope[:, half:], w_rope[:, :half]], axis=1)
    wS = jnp.concatenate([w_in[:, O_CKV:O_KR], w_in[:, O_KB:O_VB], w_rope, w_rot,
                          jnp.zeros((D, HEAD_PAD - 2 * MLA_ROPE), F32)], axis=1).astype(BF16)

    qd = MLA_NOPE + MLA_ROPE
    wq3 = w_q_b.reshape(MLA_Q_LORA, MLA_HEADS, qd)
    wqT = jnp.concatenate([wq3[:, :, :MLA_NOPE].reshape(MLA_Q_LORA, -1),
                           wq3[:, :, MLA_NOPE:MLA_NOPE + half].reshape(MLA_Q_LORA, -1),
                           wq3[:, :, MLA_NOPE + half:].reshape(MLA_Q_LORA, -1)], axis=1).astype(BF16).T
    kvd = MLA_NOPE + MLA_V
    wvT = w_kv_b.reshape(MLA_KV_LORA, MLA_HEADS, kvd)[:, :, MLA_NOPE:].reshape(MLA_KV_LORA, -1).astype(BF16).T
    knope_mask = np.zeros((1, MLA_HEADS * kvd), np.float32)
    place = np.zeros((HEAD_PAD, MLA_HEADS * HEAD_PAD), np.float32)
    for h in range(MLA_HEADS):
        knope_mask[0, h * kvd:h * kvd + MLA_NOPE] = 1.0
        place[np.arange(MLA_ROPE), h * HEAD_PAD + MLA_NOPE + np.arange(MLA_ROPE)] = 1.0
    wk = jnp.concatenate([w_kv_b * knope_mask, jnp.asarray(place)], axis=0).astype(BF16)

    qgain = q_a_gain.reshape(MLA_Q_LORA, 1)
    kvgain_col = kv_a_gain.reshape(MLA_KV_LORA, 1)
    kvgain_row = kv_a_gain.reshape(1, MLA_KV_LORA)

    freqs = ROPE_BASE ** (-jnp.arange(half, dtype=F32) / half)
    ang = jnp.arange(S, dtype=jnp.int32).astype(F32)[:, None] * freqs[None, :]
    cos, sin = jnp.cos(ang), jnp.sin(ang)
    tiles_q = lambda a: jnp.tile(a.reshape(NT, TQ, half).transpose(0, 2, 1), (1, MLA_HEADS, 1))
    cosq, sinq = tiles_q(cos), tiles_q(sin)
    zeros_k = jnp.zeros((S, HEAD_PAD - MLA_ROPE), F32)
    cosk = jnp.concatenate([cos, cos, zeros_k], axis=1)
    sink = jnp.concatenate([sin, sin, zeros_k], axis=1)

    wkm = w_mem_kv[:, :WIDTH].astype(BF16)
    wvmT = w_mem_kv[:, WIDTH:].T.astype(BF16)
    km, vmT = pl.pallas_call(
        _mem_kv_kernel,
        grid=(B,),
        in_specs=[pl.BlockSpec((1, MEM_LEN, D), lambda b: (b, 0, 0)),
                  _const_spec((D, WIDTH)), _const_spec((WIDTH, D))],
        out_specs=[pl.BlockSpec((1, MEM_LEN, WIDTH), lambda b: (b, 0, 0)),
                   pl.BlockSpec((1, WIDTH, MEM_LEN), lambda b: (b, 0, 0))],
        out_shape=[jax.ShapeDtypeStruct((B, MEM_LEN, WIDTH), BF16),
                   jax.ShapeDtypeStruct((B, WIDTH, MEM_LEN), BF16)],
        compiler_params=_params(1),
        name="mem_kv",
    )(mem, wkm, wvmT)

    TD = DENSE_TILES
    assert NT % TD == 0
    head_q = lambda: pl.BlockSpec((1, MLA_HEADS, TD, HEAD_PAD, TQ), lambda b, t: (b, 0, t, 0, 0))
    head_v = lambda rows: pl.BlockSpec((1, MLA_HEADS, TD, rows, TQ), lambda b, t: (b, 0, t, 0, 0))
    wide = lambda: pl.BlockSpec((1, TD, WIDTH, TQ), lambda b, t: (b, t, 0, 0))
    qa, ka, va, qb, kb, vb, ga, gb, om = pl.pallas_call(
        _in_proj_kernel,
        grid=(B, NT // TD),
        in_specs=[pl.BlockSpec((1, TD * TQ, D), lambda b, t: (b, t, 0)),
                  _const_spec((R_END, D)), _const_spec((D, C_END)),
                  _const_spec((MLA_Q_LORA, 1)), _const_spec((MLA_KV_LORA, 1)), _const_spec((1, MLA_KV_LORA)),
                  _const_spec((MLA_HEADS * qd, MLA_Q_LORA)), _const_spec((WIDTH, MLA_KV_LORA)),
                  _const_spec((2 * HEAD_PAD, MLA_HEADS * HEAD_PAD)),
                  pl.BlockSpec((TD, MLA_HEADS * half, TQ), lambda b, t: (t, 0, 0)),
                  pl.BlockSpec((TD, MLA_HEADS * half, TQ), lambda b, t: (t, 0, 0)),
                  pl.BlockSpec((TD * TQ, HEAD_PAD), lambda b, t: (t, 0)),
                  pl.BlockSpec((TD * TQ, HEAD_PAD), lambda b, t: (t, 0)),
                  pl.BlockSpec((1, MEM_LEN, WIDTH), lambda b, t: (b, 0, 0)),
                  pl.BlockSpec((1, WIDTH, MEM_LEN), lambda b, t: (b, 0, 0))],
        out_specs=[head_q(),
                   pl.BlockSpec((1, TD * TQ, MLA_HEADS * HEAD_PAD), lambda b, t: (b, t, 0)),
                   head_v(MLA_V_ROWS),
                   head_q(),
                   pl.BlockSpec((1, TD * TQ, WIDTH), lambda b, t: (b, t, 0)),
                   head_v(SB_HEAD_DIM),
                   wide(), wide(), wide()],
        out_shape=[jax.ShapeDtypeStruct((B, MLA_HEADS, NT, HEAD_PAD, TQ), BF16),
                   jax.ShapeDtypeStruct((B, S, MLA_HEADS * HEAD_PAD), BF16),
                   jax.ShapeDtypeStruct((B, MLA_HEADS, NT, MLA_V_ROWS, TQ), BF16),
                   jax.ShapeDtypeStruct((B, SB_HEADS, NT, HEAD_PAD, TQ), BF16),
                   jax.ShapeDtypeStruct((B, S, WIDTH), BF16),
                   jax.ShapeDtypeStruct((B, SB_HEADS, NT, SB_HEAD_DIM, TQ), BF16),
                   jax.ShapeDtypeStruct((B, NT, WIDTH, TQ), F32),
                   jax.ShapeDtypeStruct((B, NT, WIDTH, TQ), F32),
                   jax.ShapeDtypeStruct((B, NT, WIDTH, TQ), BF16)],
        compiler_params=_params(2),
        name="in_proj",
    )(x, wT, wS, qgain, kvgain_col, kvgain_row, wqT, wvT, wk, cosq, sinq, cosk, sink, km, vmT)

    G = HEADS_PER_STEP
    group = lambda rows: pl.BlockSpec((1, G, NT, rows, TQ), lambda b, h: (b, h, 0, 0, 0))
    q_spec, v_spec = group(HEAD_PAD), group(MLA_V)
    o_shape = jax.ShapeDtypeStruct((B, MLA_HEADS, NT, MLA_V, TQ), F32)
    full = [(qi, kj) for qi in range(NT) for kj in range(qi)]
    tab_np = np.zeros((4, len(full) + 1), np.int32)
    tab_np[0:2, :] = NT - 1
    tab_np[0:2, :NT] = np.arange(NT)
    tab_np[2:4, :] = np.array(full[-1])[:, None]
    tab_np[2:4, :len(full)] = np.array(full).T
    oa = pl.pallas_call(
        _mla_attn_kernel,
        grid=(B, MLA_HEADS // G),
        in_specs=[pl.BlockSpec(memory_space=pltpu.SMEM),
                  q_spec, pl.BlockSpec((1, S, G * HEAD_PAD), lambda b, h: (b, 0, h)), group(MLA_V_ROWS)],
        out_specs=v_spec,
        out_shape=o_shape,
        scratch_shapes=[pltpu.VMEM((G, TQ, TQ), F32), pltpu.VMEM((G, TQ, TQ), F32),
                        pltpu.VMEM((NT * G, 1, TQ), F32), pltpu.VMEM((NT * G, MLA_V_ROWS, TQ), F32)],
        compiler_params=_params(2),
        name="mla_attn",
    )(jnp.asarray(tab_np), qa, ka, va)

    tri_np = np.triu(np.ones((TQ, TQ), np.float32))
    tri = jnp.asarray(np.concatenate([tri_np, tri_np], axis=1), dtype=BF16)
    ob = pl.pallas_call(
        _sb_attn_kernel,
        grid=(B, SB_HEADS // G),
        in_specs=[q_spec, pl.BlockSpec((1, S, G * SB_HEAD_DIM), lambda b, h: (b, 0, h)), v_spec,
                  _const_spec((TQ, 2 * TQ))],
        out_specs=v_spec,
        out_shape=o_shape,
        compiler_params=_params(2),
        name="sb_attn",
    )(qb, kb, vb, tri)

    TD = DENSE_TILES
    assert NT % TD == 0
    o_in = lambda: pl.BlockSpec((1, MLA_HEADS, TD, MLA_V, TQ), lambda b, t: (b, 0, t, 0, 0))
    wide_d = lambda: pl.BlockSpec((1, TD, WIDTH, TQ), lambda b, t: (b, t, 0, 0))
    y = pl.pallas_call(
        _out_block_kernel,
        grid=(B, NT // TD),
        in_specs=[pl.BlockSpec((1, TD * TQ, D), lambda b, t: (b, t, 0)),
                  o_in(), o_in(), wide_d(), wide_d(), wide_d(),
                  _const_spec((WIDTH, D)), _const_spec((WIDTH, D)), _const_spec((WIDTH, D)),
                  _const_spec((D, 3 * D)), _const_spec((1, 3 * D)), _const_spec((D, D)),
                  _const_spec((1, D)), _const_spec((1, D))],
        out_specs=pl.BlockSpec((1, TD * TQ, D), lambda b, t: (b, t, 0)),
        out_shape=jax.ShapeDtypeStruct((B, S, D), F32),
        compiler_params=_params(2),
        name="out_block",
    )(x, oa, ob, ga, gb, om,
      w_branch_mla.astype(BF16), w_branch_sb.astype(BF16), w_branch_mem.astype(BF16),
      w_merge_gate.astype(BF16), b_merge_gate.reshape(1, 3 * D), w_out.astype(BF16),
      ln_gain.reshape(1, D), ln_bias.reshape(1, D))
    return y


def kernel(x, mem, w_in, w_mem_kv, q_a_gain, w_q_b, kv_a_gain, w_kv_b, w_branch_mla, w_branch_sb,
           w_branch_mem, w_merge_gate, b_merge_gate, w_out, ln_gain, ln_bias):
    h = x
    for l in range(w_in.shape[0]):
        h = _layer(h, mem, w_in[l], w_mem_kv[l], q_a_gain[l], w_q_b[l], kv_a_gain[l], w_kv_b[l],
                   w_branch_mla[l], w_branch_sb[l], w_branch_mem[l], w_merge_gate[l], b_merge_gate[l],
                   w_out[l], ln_gain[l], ln_bias[l])
    return h
```
